```python
import jax, jax.numpy as jnp
from jax import lax
import numpy as np

D_MODEL = 2048
BATCH = 4
SEQ = 4096
DEPTH = 2

CHUNK = 64
Q_BLOCK = 128
N_A = DEPTH // 2
N_B = DEPTH - N_A

RET_HEADS = 8
RET_QK = D_MODEL
RET_V = 2 * D_MODEL
RET_DK = RET_QK // RET_HEADS
RET_DV = RET_V // RET_HEADS

MLA_HEADS = 16
MLA_NOPE = 128
MLA_ROPE = 64
MLA_V = 128
Q_RANK = 512
KV_RANK = 512

D_FF = 5632
ROPE_THETA = 10000.0
EPS = 1e-6
N_MOD = 9
ADA_SCALE = 0.5

kernel_name = "yoco_retention_mla_macaron_adaln"


def rmsnorm(x, g):
    x32 = x.astype(jnp.float32)
    y = x32 * lax.rsqrt(jnp.mean(x32 * x32, axis=-1, keepdims=True) + EPS)
    return y.astype(x.dtype) * g


def modulate(x, g, shift, scale):
    return rmsnorm(x, g) * (1.0 + scale[:, None, :]) + shift[:, None, :]


def rope(x, pos):
    half = x.shape[-1] // 2
    inv = ROPE_THETA ** (-jnp.arange(half, dtype=jnp.float32) / half)
    ang = pos.astype(jnp.float32)[..., None] * inv
    cos = jnp.cos(ang)[:, :, None, :].astype(x.dtype)
    sin = jnp.sin(ang)[:, :, None, :].astype(x.dtype)
    x1, x2 = x[..., :half], x[..., half:]
    return jnp.concatenate([x1 * cos - x2 * sin, x1 * sin + x2 * cos], axis=-1)


def swiglu(h, w_in, w_out):
    gate, up = jnp.split(h @ w_in, 2, axis=-1)
    return (jax.nn.silu(gate) * up) @ w_out


def chunk_retention(q, k, v):
    B, S, H, dk = q.shape
    dv = v.shape[-1]
    nc = S // CHUNK
    dt = q.dtype
    log_g = jnp.log1p(-(2.0 ** (-5.0 - jnp.arange(H, dtype=jnp.float32))))
    idx = jnp.arange(CHUNK, dtype=jnp.float32)
    d_intra = jnp.exp(log_g[:, None, None] * jnp.abs(idx[:, None] - idx[None, :])).astype(dt)
    xi = jnp.exp(log_g[:, None] * (idx + 1.0)).astype(dt)
    zeta = jnp.exp(log_g[:, None] * (CHUNK - 1.0 - idx)).astype(dt)
    g_chunk = jnp.exp(log_g * CHUNK).astype(dt)

    def to_chunks(t):
        return t.reshape(B, nc, CHUNK, H, t.shape[-1]).transpose(1, 0, 3, 2, 4)

    def step(state, inp):
        qc, kc, vc = inp
        s = jnp.einsum('bhid,bhjd->bhij', qc, kc) * d_intra
        o = (jnp.einsum('bhij,bhjv->bhiv', s, vc)
             + jnp.einsum('bhid,bhdv->bhiv', qc * xi[None, :, :, None], state))
        state = (state * g_chunk[None, :, None, None]
                 + jnp.einsum('bhjd,bhjv->bhdv', kc * zeta[None, :, :, None], vc))
        return state, o

    state0 = jnp.zeros((B, H, dk, dv), dt)
    _, o = lax.scan(step, state0, (to_chunks(q), to_chunks(k), to_chunks(v)))
    return o.transpose(1, 0, 3, 2, 4).reshape(B, S, H, dv)


def retention_mixer(h, pos, w_in, gn_g, w_out):
    B, S, _ = h.shape
    q, k, v, g = jnp.split(h @ w_in, [RET_QK, 2 * RET_QK, 2 * RET_QK + RET_V], axis=-1)
    q = rope(q.reshape(B, S, RET_HEADS, RET_DK), pos) * (RET_DK ** -0.5)
    k = rope(k.reshape(B, S, RET_HEADS, RET_DK), pos)
    v = v.reshape(B, S, RET_HEADS, RET_DV)
    o = chunk_retention(q, k, v).astype(jnp.float32)
    mu = jnp.mean(o, axis=-1, keepdims=True)
    var = jnp.mean(jnp.square(o - mu), axis=-1, keepdims=True)
    on = ((o - mu) * lax.rsqrt(var + EPS)).reshape(B, S, RET_V).astype(h.dtype) * gn_g
    return (jax.nn.silu(g) * on) @ w_out


def shared_kv(x, c, pos, kv_ada_w, kv_ada_b, kv_norm_g, w_dkv, kv_latent_g, w_ukv):
    B, S, _ = x.shape
    shift, scale = jnp.split(jax.nn.silu(c) @ kv_ada_w + kv_ada_b, 2, axis=-1)
    hk = modulate(x, kv_norm_g, shift, scale)
    ckv, kr = jnp.split(hk @ w_dkv, [KV_RANK], axis=-1)
    ckv = rmsnorm(ckv, kv_latent_g)
    kr = rope(kr[:, :, None, :], pos)[:, :, 0, :]
    kv = (ckv @ w_ukv).reshape(B, S, MLA_HEADS, MLA_NOPE + MLA_V)
    kn, v = jnp.split(kv, [MLA_NOPE], axis=-1)
    return kn, kr, v


def mla_mixer(h, pos, kn, kr, v, w_dq, q_latent_g, w_uq, w_out):
    B, S, _ = h.shape
    cq = rmsnorm(h @ w_dq, q_latent_g)
    q = (cq @ w_uq).reshape(B, S, MLA_HEADS, MLA_NOPE + MLA_ROPE)
    qn, qr = jnp.split(q, [MLA_NOPE], axis=-1)
    qr = rope(qr, pos)
    scale = (MLA_NOPE + MLA_ROPE) ** -0.5
    nb = S // Q_BLOCK
    qn_b = qn.reshape(B, nb, Q_BLOCK, MLA_HEADS, MLA_NOPE).transpose(1, 0, 2, 3, 4)
    qr_b = qr.reshape(B, nb, Q_BLOCK, MLA_HEADS, MLA_ROPE).transpose(1, 0, 2, 3, 4)
    k_chunk = jnp.arange(S) // CHUNK

    def one_block(args):
        qnb, qrb, bi = args
        s = (jnp.einsum('bqhd,bkhd->bhqk', qnb, kn)
             + jnp.einsum('bqhr,bkr->bhqk', qrb, kr)).astype(jnp.float32) * scale
        q_chunk = (bi * Q_BLOCK + jnp.arange(Q_BLOCK)) // CHUNK
        mask = k_chunk[None, :] <= q_chunk[:, None]
        s = jnp.where(mask[None, None], s, -1e30)
        p = jax.nn.softmax(s, axis=-1).astype(v.dtype)
        return jnp.einsum('bhqk,bkhd->bqhd', p, v)

    o = lax.map(one_block, (qn_b, qr_b, jnp.arange(nb)))
    o = o.transpose(1, 0, 2, 3, 4).reshape(B, S, MLA_HEADS * MLA_V)
    return o @ w_out


def setup_inputs(seed: int = 0) -> dict:
    key = jax.random.key(seed)
    ks = jax.random.split(key, 24)
    f32 = jnp.float32
    D = D_MODEL

    def nrm(k, shape, fan_in, mult=1.0):
        return jax.random.normal(k, shape, f32) * (mult * fan_in ** -0.5)

    def gain(k, shape):
        return 1.0 + 0.02 * jax.random.normal(k, shape, f32)

    x = jax.random.normal(ks[0], (BATCH, SEQ, D), f32)
    c = jax.random.normal(ks[1], (BATCH, D), f32)
    offset = jax.random.randint(ks[2], (BATCH, 1), 0, 1024, dtype=jnp.int32)
    positions = offset + jnp.arange(SEQ, dtype=jnp.int32)[None, :]
    return {
        "x": x,
        "c": c,
        "positions": positions,
        "ada_w": nrm(ks[3], (DEPTH, D, N_MOD * D), D, ADA_SCALE),
        "ada_b": 0.02 * jax.random.normal(ks[4], (DEPTH, N_MOD * D), f32),
        "norm_g": gain(ks[5], (DEPTH, 3, D)),
        "ffn_w_in": nrm(ks[6], (DEPTH, 2, D, 2 * D_FF), D),
        "ffn_w_out": nrm(ks[7], (DEPTH, 2, D_FF, D), D_FF),
        "ret_w_in": nrm(ks[8], (N_A, D, 2 * RET_QK + 2 * RET_V), D),
        "ret_gn_g": gain(ks[9], (N_A, RET_V)),
        "ret_w_out": nrm(ks[10], (N_A, RET_V, D), RET_V),
        "kv_ada_w": nrm(ks[11], (D, 2 * D), D, ADA_SCALE),
        "kv_ada_b": 0.02 * jax.random.normal(ks[12], (2 * D,), f32),
        "kv_norm_g": gain(ks[13], (D,)),
        "mla_w_dkv": nrm(ks[14], (D, KV_RANK + MLA_ROPE), D),
        "kv_latent_g": gain(ks[15], (KV_RANK,)),
        "mla_w_ukv": nrm(ks[16], (KV_RANK, MLA_HEADS * (MLA_NOPE + MLA_V)), KV_RANK),
        "mla_w_dq": nrm(ks[17], (N_B, D, Q_RANK), D),
        "q_latent_g": gain(ks[18], (N_B, Q_RANK)),
        "mla_w_uq": nrm(ks[19], (N_B, Q_RANK, MLA_HEADS * (MLA_NOPE + MLA_ROPE)), Q_RANK),
        "mla_w_out": nrm(ks[20], (N_B, MLA_HEADS * MLA_V, D), MLA_HEADS * MLA_V),
        "final_g": gain(ks[21], (D,)),
    }


def reference(x, c, positions, ada_w, ada_b, norm_g, ffn_w_in, ffn_w_out,
              ret_w_in, ret_gn_g, ret_w_out, kv_ada_w, kv_ada_b, kv_norm_g,
              mla_w_dkv, kv_latent_g, mla_w_ukv, mla_w_dq, q_latent_g, mla_w_uq,
              mla_w_out, final_g):
    c_act = jax.nn.silu(c)
    kn = kr = v = None
    for l in range(DEPTH):
        mods = jnp.split(c_act @ ada_w[l] + ada_b[l], N_MOD, axis=-1)
        sh1, sc1, gt1, shm, scm, gtm, sh2, sc2, gt2 = mods
        if l == N_A:
            kn, kr, v = shared_kv(x, c, positions, kv_ada_w, kv_ada_b, kv_norm_g,
                                  mla_w_dkv, kv_latent_g, mla_w_ukv)
        h = modulate(x, norm_g[l, 0], sh1, sc1)
        x = x + 0.5 * gt1[:, None, :] * swiglu(h, ffn_w_in[l, 0], ffn_w_out[l, 0])
        h = modulate(x, norm_g[l, 1], shm, scm)
        if l < N_A:
            y = retention_mixer(h, positions, ret_w_in[l], ret_gn_g[l], ret_w_out[l])
        else:
            j = l - N_A
            y = mla_mixer(h, positions, kn, kr, v, mla_w_dq[j], q_latent_g[j],
                          mla_w_uq[j], mla_w_out[j])
        x = x + gtm[:, None, :] * y
        h = modulate(x, norm_g[l, 2], sh2, sc2)
        x = x + 0.5 * gt2[:, None, :] * swiglu(h, ffn_w_in[l, 1], ffn_w_out[l, 1])
    return rmsnorm(x, final_g)
```

```python
import functools
import math

import jax
import jax.numpy as jnp
from jax import lax
from jax.experimental import pallas as pl
from jax.experimental.pallas import tpu as pltpu

CHUNK = 64
RET_HEADS = 8
MLA_HEADS = 16
MLA_NOPE = 128
MLA_ROPE = 64
MLA_V = 128
Q_RANK = 512
KV_RANK = 512
ROPE_THETA = 10000.0
EPS = 1e-6
N_MOD = 9

BF16 = jnp.bfloat16
F32 = jnp.float32

VMEM_LIMIT_BYTES = 56 * 1024 * 1024


def _params(*sem):
    return pltpu.CompilerParams(dimension_semantics=sem, vmem_limit_bytes=VMEM_LIMIT_BYTES)


def _silu(x):
    return x / (1.0 + jnp.exp(-x))


def _dot(a, b):
    return jnp.dot(a, b, preferred_element_type=F32)


def _modulated_norm(x, g, shift, scale):
    ms = jnp.mean(x * x, axis=-1, keepdims=True)
    return (x * lax.rsqrt(ms + EPS)) * g * (1.0 + scale) + shift


def _rope_tables(pos_ref, inv_ref):
    ang = pos_ref[...].astype(F32) * inv_ref[...]
    return jnp.cos(ang), jnp.sin(ang)


def _mod_spec(k, nb, blocks_per_batch, d):
    return pl.BlockSpec((None, 1, d), lambda i, *_: (k * nb + i // blocks_per_batch, 0, 0))


def _mods_kernel(c_ref, w_ref, b_ref, o_ref):
    ca = _silu(c_ref[...]).astype(BF16)
    o_ref[...] = _dot(ca, w_ref[...].astype(BF16)) + b_ref[...]


def _mods(c_pad, w, b, tn=1024):
    rows, d = c_pad.shape
    n = w.shape[1]
    return pl.pallas_call(
        _mods_kernel,
        grid=(n // tn,),
        in_specs=[
            pl.BlockSpec((rows, d), lambda j: (0, 0)),
            pl.BlockSpec((d, tn), lambda j: (0, j)),
            pl.BlockSpec((1, tn), lambda j: (0, j)),
        ],
        out_specs=pl.BlockSpec((rows, tn), lambda j: (0, j)),
        out_shape=jax.ShapeDtypeStruct((rows, n), F32),
        compiler_params=_params("arbitrary"),
        name="ada_mods",
    )(c_pad, w, b.reshape(1, n))


def _mod_table(m, nb, d):
    k = m.shape[1] // d
    return m[:nb].reshape(nb, k, d).transpose(1, 0, 2).reshape(k * nb, 1, d)


def _ffn_kernel(x_ref, sh_ref, sc_ref, gt_ref, g_ref, wg_ref, wu_ref, wo_ref, fg_ref, o_ref, h_ref,
                *, final_norm):
    j = pl.program_id(1)

    @pl.when(j == 0)
    def _():
        h = _modulated_norm(x_ref[...], g_ref[...], sh_ref[...], sc_ref[...])
        h_ref[...] = h.astype(BF16)
        o_ref[...] = jnp.zeros_like(o_ref)

    h = h_ref[...]
    gate = _dot(h, wg_ref[...])
    up = _dot(h, wu_ref[...])
    a = (_silu(gate) * up).astype(BF16)
    o_ref[...] += _dot(a, wo_ref[...])

    @pl.when(j == pl.num_programs(1) - 1)
    def _():
        y = x_ref[...] + (0.5 * gt_ref[...]) * o_ref[...]
        if final_norm:
            ms = jnp.mean(y * y, axis=-1, keepdims=True)
            y = (y * lax.rsqrt(ms + EPS)) * fg_ref[...]
        o_ref[...] = y


def _ffn(x, mods, mod_base, nb, norm_g, w_in, w_out, final_g, final_norm, tm=512, tf=512):
    t, d = x.shape
    dff = w_out.shape[0]
    nf = dff // tf
    bpb = (t // nb) // tm
    return pl.pallas_call(
        functools.partial(_ffn_kernel, final_norm=final_norm),
        grid=(t // tm, nf),
        in_specs=[
            pl.BlockSpec((tm, d), lambda i, j: (i, 0)),
            _mod_spec(mod_base + 0, nb, bpb, d),
            _mod_spec(mod_base + 1, nb, bpb, d),
            _mod_spec(mod_base + 2, nb, bpb, d),
            pl.BlockSpec((1, d), lambda i, j: (0, 0)),
            pl.BlockSpec((d, tf), lambda i, j: (0, j)),
            pl.BlockSpec((d, tf), lambda i, j: (0, nf + j)),
            pl.BlockSpec((tf, d), lambda i, j: (j, 0)),
            pl.BlockSpec((1, d), lambda i, j: (0, 0)),
        ],
        out_specs=pl.BlockSpec((tm, d), lambda i, j: (i, 0)),
        out_shape=jax.ShapeDtypeStruct((t, d), F32),
        scratch_shapes=[pltpu.VMEM((tm, d), BF16)],
        compiler_params=_params("arbitrary", "arbitrary"),
        name="ffn",
    )(x, mods, mods, mods, norm_g.reshape(1, d), w_in, w_in, w_out, final_g.reshape(1, d))


def _ret_in_kernel(x_ref, sh_ref, sc_ref, g_ref, pos_ref, inv_ref, w_ref, o_ref, h_ref, cos_ref, sin_ref,
                   *, n_q_tiles, dk, q_scale):
    j = pl.program_id(1)

    @pl.when(j == 0)
    def _():
        h = _modulated_norm(x_ref[...], g_ref[...], sh_ref[...], sc_ref[...])
        h_ref[...] = h.astype(BF16)
        cos, sin = _rope_tables(pos_ref, inv_ref)
        cos_ref[...] = cos
        sin_ref[...] = sin

    y = _dot(h_ref[...], w_ref[...])
    tn = y.shape[1]
    half = dk // 2

    def roped(scale):
        cos = cos_ref[...]
        sin = sin_ref[...]
        for hd in range(tn // dk):
            x1 = y[:, hd * dk:hd * dk + half]
            x2 = y[:, hd * dk + half:(hd + 1) * dk]
            o_ref[:, hd * dk:hd * dk + half] = ((x1 * cos - x2 * sin) * scale).astype(BF16)
            o_ref[:, hd * dk + half:(hd + 1) * dk] = ((x1 * sin + x2 * cos) * scale).astype(BF16)

    @pl.when(j < n_q_tiles)
    def _():
        roped(q_scale)

    @pl.when(jnp.logical_and(j >= n_q_tiles, j < 2 * n_q_tiles))
    def _():
        roped(1.0)

    @pl.when(j >= 2 * n_q_tiles)
    def _():
        o_ref[...] = y.astype(BF16)


def _ret_in(x, mods, mod_base, nb, norm_g, pos, inv, w, qk_dim, dk, tm=512, tn=512):
    t, d = x.shape
    n = w.shape[1]
    bpb = (t // nb) // tm
    half = dk // 2
    return pl.pallas_call(
        functools.partial(_ret_in_kernel, n_q_tiles=qk_dim // tn, dk=dk, q_scale=dk ** -0.5),
        grid=(t // tm, n // tn),
        in_specs=[
            pl.BlockSpec((tm, d), lambda i, j: (i, 0)),
            _mod_spec(mod_base + 0, nb, bpb, d),
            _mod_spec(mod_base + 1, nb, bpb, d),
            pl.BlockSpec((1, d), lambda i, j: (0, 0)),
            pl.BlockSpec((tm, 1), lambda i, j: (i, 0)),
            pl.BlockSpec((1, half), lambda i, j: (0, 0)),
            pl.BlockSpec((d, tn), lambda i, j: (0, j)),
        ],
        out_specs=pl.BlockSpec((tm, tn), lambda i, j: (i, j)),
        out_shape=jax.ShapeDtypeStruct((t, n), BF16),
        scratch_shapes=[pltpu.VMEM((tm, d), BF16), pltpu.VMEM((tm, half), F32), pltpu.VMEM((tm, half), F32)],
        compiler_params=_params("arbitrary", "arbitrary"),
        name="ret_in",
    )(x, mods, mods, norm_g.reshape(1, d), pos, inv, w)


def _ret_core_kernel(lg_ref, q_ref, k_ref, v_ref, g_ref, gn_ref, o_ref, state_ref, dmask_ref, *, span):
    i = pl.program_id(2)
    lg = lg_ref[...][:, :1]

    @pl.when(i == 0)
    def _():
        state_ref[...] = jnp.zeros_like(state_ref)
        r = lax.broadcasted_iota(jnp.int32, (span, span), 0)
        c = lax.broadcasted_iota(jnp.int32, (span, span), 1)
        decay = jnp.exp(lg * jnp.abs(r - c).astype(F32))
        dmask_ref[...] = jnp.where((c // CHUNK) <= (r // CHUNK), decay, 0.0)

    idx = lax.broadcasted_iota(jnp.int32, (span, 1), 0).astype(F32)
    xi = jnp.exp(lg * (idx + 1.0))
    zeta = jnp.exp(lg * (span - 1.0 - idx))
    g_span = jnp.exp(lg * float(span))

    q = q_ref[...]
    k = k_ref[...]
    v = v_ref[...]
    s = lax.dot_general(q, k, (((1,), (1,)), ((), ())), preferred_element_type=F32) * dmask_ref[...]
    state = state_ref[...]
    o = _dot(s.astype(BF16), v) + _dot((q.astype(F32) * xi).astype(BF16), state.astype(BF16))
    kz = (k.astype(F32) * zeta).astype(BF16)
    state_ref[...] = state * g_span + lax.dot_general(kz, v, (((0,), (0,)), ((), ())),
                                                      preferred_element_type=F32)

    mu = jnp.mean(o, axis=-1, keepdims=True)
    oc = o - mu
    var = jnp.mean(oc * oc, axis=-1, keepdims=True)
    on = (oc * lax.rsqrt(var + EPS)) * gn_ref[...]
    o_ref[...] = (_silu(g_ref[...].astype(F32)) * on).astype(BF16)


def _ret_core(qkvg, log_g, gn_g, nb, seq, heads, dk, dv, span=512):
    t = qkvg.shape[0]
    ns = seq // span
    k_off = (heads * dk) // dk
    v_off = (2 * heads * dk) // dv
    g_off = v_off + heads
    row = lambda b, h, i: b * ns + i
    return pl.pallas_call(
        functools.partial(_ret_core_kernel, span=span),
        grid=(nb, heads, ns),
        in_specs=[
            pl.BlockSpec((None, 1, 128), lambda b, h, i: (h, 0, 0)),
            pl.BlockSpec((span, dk), lambda b, h, i: (row(b, h, i), h)),
            pl.BlockSpec((span, dk), lambda b, h, i: (row(b, h, i), k_off + h)),
            pl.BlockSpec((span, dv), lambda b, h, i: (row(b, h, i), v_off + h)),
            pl.BlockSpec((span, dv), lambda b, h, i: (row(b, h, i), g_off + h)),
            pl.BlockSpec((1, dv), lambda b, h, i: (0, h)),
        ],
        out_specs=pl.BlockSpec((span, dv), lambda b, h, i: (row(b, h, i), h)),
        out_shape=jax.ShapeDtypeStruct((t, heads * dv), BF16),
        scratch_shapes=[pltpu.VMEM((dk, dv), F32), pltpu.VMEM((span, span), F32)],
        compiler_params=_params("arbitrary", "arbitrary", "arbitrary"),
        name="ret_core",
    )(log_g, qkvg, qkvg, qkvg, qkvg, gn_g.reshape(1, heads * dv))


def _proj_res_kernel(y_ref, w_ref, x_ref, gt_ref, o_ref):
    o_ref[...] = x_ref[...] + gt_ref[...] * _dot(y_ref[...], w_ref[...])


def _proj_res(y, w, x, mods, mod_idx, nb, tm=512, tn=512):
    t, kdim = y.shape
    d = w.shape[1]
    bpb = (t // nb) // tm
    return pl.pallas_call(
        _proj_res_kernel,
        grid=(t // tm, d // tn),
        in_specs=[
            pl.BlockSpec((tm, kdim), lambda i, j: (i, 0)),
            pl.BlockSpec((kdim, tn), lambda i, j: (0, j)),
            pl.BlockSpec((tm, tn), lambda i, j: (i, j)),
            pl.BlockSpec((None, 1, tn), lambda i, j: (mod_idx * nb + i // bpb, 0, j)),
        ],
        out_specs=pl.BlockSpec((tm, tn), lambda i, j: (i, j)),
        out_shape=jax.ShapeDtypeStruct((t, d), F32),
        compiler_params=_params("arbitrary", "arbitrary"),
        name="proj_res",
    )(y, w, x, mods)


def _rope_tail(x, cos, sin):
    half = MLA_ROPE // 2
    rot = jnp.concatenate([-x[:, half:], x[:, :half]], axis=-1)
    return x * cos + rot * sin


def _kv_kernel(x_ref, sh_ref, sc_ref, g_ref, pos_ref, inv_ref, wdc_ref, wdr_ref, lg_ref, wu_ref, k_ref, v_ref):
    h = _modulated_norm(x_ref[...], g_ref[...], sh_ref[...], sc_ref[...]).astype(BF16)
    ckv = _dot(h, wdc_ref[...])
    ms = jnp.mean(ckv * ckv, axis=-1, keepdims=True)
    ckv = ((ckv * lax.rsqrt(ms + EPS)) * lg_ref[...]).astype(BF16)
    cos, sin = _rope_tables(pos_ref, inv_ref)
    kr = _rope_tail(_dot(h, wdr_ref[...]), cos, sin).astype(BF16)
    width = MLA_NOPE + MLA_V
    for hd in range(MLA_HEADS):
        kv = _dot(ckv, wu_ref[:, hd * width:(hd + 1) * width])
        k_ref[hd, :, :MLA_NOPE] = kv[:, :MLA_NOPE].astype(BF16)
        k_ref[hd, :, MLA_NOPE:] = kr
        v_ref[hd] = kv[:, MLA_NOPE:].astype(BF16)


def _shared_kv(x, mods, nb, seq, norm_g, pos, inv, w_dc, w_dr, latent_g, w_ukv, tm=512):
    t, d = x.shape
    spb = seq // tm
    const = lambda i: (0, 0)
    return pl.pallas_call(
        _kv_kernel,
        grid=(t // tm,),
        in_specs=[
            pl.BlockSpec((tm, d), lambda i: (i, 0)),
            _mod_spec(0, nb, spb, d),
            _mod_spec(1, nb, spb, d),
            pl.BlockSpec((1, d), const),
            pl.BlockSpec((tm, 1), lambda i: (i, 0)),
            pl.BlockSpec((1, MLA_ROPE), const),
            pl.BlockSpec(w_dc.shape, const),
            pl.BlockSpec(w_dr.shape, const),
            pl.BlockSpec((1, KV_RANK), const),
            pl.BlockSpec(w_ukv.shape, const),
        ],
        out_specs=[
            pl.BlockSpec((None, MLA_HEADS, tm, MLA_NOPE + MLA_ROPE), lambda i: (i // spb, 0, i % spb, 0)),
            pl.BlockSpec((None, MLA_HEADS, tm, MLA_V), lambda i: (i // spb, 0, i % spb, 0)),
        ],
        out_shape=[
            jax.ShapeDtypeStruct((nb, MLA_HEADS, seq, MLA_NOPE + MLA_ROPE), BF16),
            jax.ShapeDtypeStruct((nb, MLA_HEADS, seq, MLA_V), BF16),
        ],
        compiler_params=_params("arbitrary"),
        name="mla_kv",
    )(x, mods, mods, norm_g.reshape(1, d), pos, inv, w_dc, w_dr, latent_g.reshape(1, KV_RANK), w_ukv)


def _q_kernel(x_ref, sh_ref, sc_ref, g_ref, pos_ref, inv_ref, wd_ref, lg_ref, wu_ref, q_ref, *, q_scale):
    h = _modulated_norm(x_ref[...], g_ref[...], sh_ref[...], sc_ref[...]).astype(BF16)
    cq = _dot(h, wd_ref[...])
    ms = jnp.mean(cq * cq, axis=-1, keepdims=True)
    cq = ((cq * lax.rsqrt(ms + EPS)) * lg_ref[...]).astype(BF16)
    cos, sin = _rope_tables(pos_ref, inv_ref)
    for hd in range(MLA_HEADS):
        qh = _dot(cq, wu_ref[hd]) * q_scale
        q_ref[hd, :, :MLA_NOPE] = qh[:, :MLA_NOPE].astype(BF16)
        q_ref[hd, :, MLA_NOPE:] = _rope_tail(qh[:, MLA_NOPE:], cos, sin).astype(BF16)


def _mla_q(x, mods, mod_base, nb, seq, norm_g, pos, inv, w_dq, latent_g, w_uq_heads, tm=512):
    t, d = x.shape
    spb = seq // tm
    const = lambda i: (0, 0)
    dq = MLA_NOPE + MLA_ROPE
    return pl.pallas_call(
        functools.partial(_q_kernel, q_scale=dq ** -0.5),
        grid=(t // tm,),
        in_specs=[
            pl.BlockSpec((tm, d), lambda i: (i, 0)),
            _mod_spec(mod_base + 0, nb, spb, d),
            _mod_spec(mod_base + 1, nb, spb, d),
            pl.BlockSpec((1, d), const),
            pl.BlockSpec((tm, 1), lambda i: (i, 0)),
            pl.BlockSpec((1, MLA_ROPE), const),
            pl.BlockSpec(w_dq.shape, const),
            pl.BlockSpec((1, Q_RANK), const),
            pl.BlockSpec(w_uq_heads.shape, lambda i: (0, 0, 0)),
        ],
        out_specs=pl.BlockSpec((None, MLA_HEADS, tm, dq), lambda i: (i // spb, 0, i % spb, 0)),
        out_shape=jax.ShapeDtypeStruct((nb, MLA_HEADS, seq, dq), BF16),
        compiler_params=_params("arbitrary"),
        name="mla_q",
    )(x, mods, mods, norm_g.reshape(1, d), pos, inv, w_dq, latent_g.reshape(1, Q_RANK), w_uq_heads)


def _attn_kernel(q_ref, k_ref, v_ref, o_ref, *, tq, tk):
    qi = pl.program_id(2)
    q = q_ref[...]
    q_chunk = (qi * tq + lax.broadcasted_iota(jnp.int32, (tq, tk), 0)) // CHUNK
    k_iota = lax.broadcasted_iota(jnp.int32, (tq, tk), 1)
    n_kv = ((qi + 1) * tq + tk - 1) // tk

    def body(kb, carry):
        m, l, acc = carry
        start = pl.multiple_of(kb * tk, tk)
        kt = k_ref[pl.ds(start, tk), :]
        vt = v_ref[pl.ds(start, tk), :]
        s = lax.dot_general(q, kt, (((1,), (1,)), ((), ())), preferred_element_type=F32)
        s = jnp.where((kb * tk + k_iota) // CHUNK <= q_chunk, s, -1e30)
        m_new = jnp.maximum(m, jnp.max(s, axis=-1, keepdims=True))
        alpha = jnp.exp(m - m_new)
        p = jnp.exp(s - m_new)
        l = alpha * l + jnp.sum(p, axis=-1, keepdims=True)
        acc = alpha * acc + _dot(p.astype(BF16), vt)
        return m_new, l, acc

    m0 = jnp.full((tq, 1), -1e30, F32)
    l0 = jnp.zeros((tq, 1), F32)
    a0 = jnp.zeros((tq, v_ref.shape[-1]), F32)
    _, l, acc = lax.fori_loop(0, n_kv, body, (m0, l0, a0))
    o_ref[...] = (acc / l).astype(BF16)


def _attention(q, k, v, tq=256, tk=256):
    nb, heads, seq, dq = q.shape
    dv = v.shape[-1]
    nq = seq // tq
    return pl.pallas_call(
        functools.partial(_attn_kernel, tq=tq, tk=tk),
        grid=(nb, heads, nq),
        in_specs=[
            pl.BlockSpec((None, None, tq, dq), lambda b, h, i: (b, h, i, 0)),
            pl.BlockSpec((None, None, seq, dq), lambda b, h, i: (b, h, 0, 0)),
            pl.BlockSpec((None, None, seq, dv), lambda b, h, i: (b, h, 0, 0)),
        ],
        out_specs=pl.BlockSpec((tq, dv), lambda b, h, i: (b * nq + i, h)),
        out_shape=jax.ShapeDtypeStruct((nb * seq, heads * dv), BF16),
        compiler_params=_params("arbitrary", "arbitrary", "arbitrary"),
        name="mla_attn",
    )(q, k, v)


def kernel(x, c, positions, ada_w, ada_b, norm_g, ffn_w_in, ffn_w_out, ret_w_in, ret_gn_g, ret_w_out, kv_ada_w, kv_ada_b, kv_norm_g, mla_w_dkv, kv_latent_g, mla_w_ukv, mla_w_dq, q_latent_g, mla_w_uq, mla_w_out, final_g):
    nb, seq, d = x.shape
    t = nb * seq
    depth = ada_w.shape[0]
    n_ret = ret_w_in.shape[0]
    ret_qk = d
    ret_dk = ret_qk // RET_HEADS
    ret_dv = ret_w_out.shape[1] // RET_HEADS

    xs = x.reshape(t, d)
    pos = positions.reshape(t, 1)
    c_pad = jnp.zeros((8, d), F32).at[:nb].set(c)

    half_ret = ret_dk // 2
    inv_ret = (ROPE_THETA ** (-jnp.arange(half_ret, dtype=F32) / half_ret)).reshape(1, half_ret)
    half_mla = MLA_ROPE // 2
    inv_mla_half = ROPE_THETA ** (-jnp.arange(half_mla, dtype=F32) / half_mla)
    inv_mla = jnp.concatenate([inv_mla_half, inv_mla_half]).reshape(1, MLA_ROPE)
    log_g = jnp.log1p(-(2.0 ** (-5.0 - jnp.arange(RET_HEADS, dtype=F32))))
    log_g = jnp.broadcast_to(log_g[:, None, None], (RET_HEADS, 1, 128))

    kq = kk = vv = None
    for l in range(depth):
        mods = _mod_table(_mods(c_pad, ada_w[l], ada_b[l]), nb, d)
        if l == n_ret:
            kv_mods = _mod_table(_mods(c_pad, kv_ada_w, kv_ada_b), nb, d)
            w_dkv = mla_w_dkv.astype(BF16)
            kk, vv = _shared_kv(xs, kv_mods, nb, seq, kv_norm_g, pos, inv_mla,
                                w_dkv[:, :KV_RANK], w_dkv[:, KV_RANK:], kv_latent_g, mla_w_ukv.astype(BF16))
        xs = _ffn(xs, mods, 0, nb, norm_g[l, 0], ffn_w_in[l, 0].astype(BF16), ffn_w_out[l, 0].astype(BF16),
                  final_g, False)
        if l < n_ret:
            qkvg = _ret_in(xs, mods, 3, nb, norm_g[l, 1], pos, inv_ret, ret_w_in[l].astype(BF16), ret_qk, ret_dk)
            y = _ret_core(qkvg, log_g, ret_gn_g[l], nb, seq, RET_HEADS, ret_dk, ret_dv)
            xs = _proj_res(y, ret_w_out[l].astype(BF16), xs, mods, 5, nb)
        else:
            jj = l - n_ret
            dq = MLA_NOPE + MLA_ROPE
            w_uq = mla_w_uq[jj].astype(BF16).reshape(Q_RANK, MLA_HEADS, dq).transpose(1, 0, 2)
            kq = _mla_q(xs, mods, 3, nb, seq, norm_g[l, 1], pos, inv_mla, mla_w_dq[jj].astype(BF16),
                        q_latent_g[jj], w_uq)
            y = _attention(kq, kk, vv)
            xs = _proj_res(y, mla_w_out[jj].astype(BF16), xs, mods, 5, nb)
        xs = _ffn(xs, mods, 6, nb, norm_g[l, 2], ffn_w_in[l, 1].astype(BF16), ffn_w_out[l, 1].astype(BF16),
                  final_g, l == depth - 1)
    return xs.reshape(nb, seq, d)
```

```python
import functools
import math

import jax
import jax.numpy as jnp
from jax import lax
from jax.experimental import pallas as pl
from jax.experimental.pallas import tpu as pltpu

CHUNK = 64
RET_HEADS = 8
MLA_HEADS = 16
MLA_NOPE = 128
MLA_ROPE = 64
MLA_V = 128
Q_RANK = 512
KV_RANK = 512
ROPE_THETA = 10000.0
EPS = 1e-6
N_MOD = 9

BF16 = jnp.bfloat16
F32 = jnp.float32

VMEM_LIMIT_BYTES = 56 * 1024 * 1024


def _params(*sem):
    return pltpu.CompilerParams(dimension_semantics=sem, vmem_limit_bytes=VMEM_LIMIT_BYTES)


def _silu(x):
    return x / (1.0 + jnp.exp(-x))


def _dot(a, b):
    return jnp.dot(a, b, preferred_element_type=F32)


def _modulated_norm(x, g, shift, scale):
    ms = jnp.mean(x * x, axis=-1, keepdims=True)
    return (x * lax.rsqrt(ms + EPS)) * g * (1.0 + scale) + shift


def _rope_tables(pos_ref, inv_ref):
    ang = pos_ref[...].astype(F32) * inv_ref[...]
    return jnp.cos(ang), jnp.sin(ang)


def _mod_spec(k, nb, blocks_per_batch, d):
    return pl.BlockSpec((None, 1, d), lambda i, *_: (k * nb + i // blocks_per_batch, 0, 0))


def _mods_kernel(c_ref, w_ref, b_ref, o_ref):
    ca = _silu(c_ref[...]).astype(BF16)
    o_ref[...] = _dot(ca, w_ref[...].astype(BF16)) + b_ref[...]


def _mods(c_pad, w, b, tn=1024):
    rows, d = c_pad.shape
    n = w.shape[1]
    return pl.pallas_call(
        _mods_kernel,
        grid=(n // tn,),
        in_specs=[
            pl.BlockSpec((rows, d), lambda j: (0, 0)),
            pl.BlockSpec((d, tn), lambda j: (0, j)),
            pl.BlockSpec((1, tn), lambda j: (0, j)),
        ],
        out_specs=pl.BlockSpec((rows, tn), lambda j: (0, j)),
        out_shape=jax.ShapeDtypeStruct((rows, n), F32),
        compiler_params=_params("arbitrary"),
        name="ada_mods",
    )(c_pad, w, b.reshape(1, n))


def _mod_table(m, nb, d):
    k = m.shape[1] // d
    return m[:nb].reshape(nb, k, d).transpose(1, 0, 2).reshape(k * nb, 1, d)


def _ffn_kernel(x_ref, sh_ref, sc_ref, gt_ref, g_ref, wg_ref, wu_ref, wo_ref, fg_ref, o_ref, h_ref,
                *, final_norm):
    j = pl.program_id(1)

    @pl.when(j == 0)
    def _():
        h = _modulated_norm(x_ref[...], g_ref[...], sh_ref[...], sc_ref[...])
        h_ref[...] = h.astype(BF16)
        o_ref[...] = jnp.zeros_like(o_ref)

    h = h_ref[...]
    gate = _dot(h, wg_ref[...])
    up = _dot(h, wu_ref[...])
    a = (_silu(gate) * up).astype(BF16)
    o_ref[...] += _dot(a, wo_ref[...])

    @pl.when(j == pl.num_programs(1) - 1)
    def _():
        y = x_ref[...] + (0.5 * gt_ref[...]) * o_ref[...]
        if final_norm:
            ms = jnp.mean(y * y, axis=-1, keepdims=True)
            y = (y * lax.rsqrt(ms + EPS)) * fg_ref[...]
        o_ref[...] = y


def _ffn(x, mods, mod_base, nb, norm_g, w_in, w_out, final_g, final_norm, tm=512, tf=512):
    t, d = x.shape
    dff = w_out.shape[0]
    nf = dff // tf
    bpb = (t // nb) // tm
    return pl.pallas_call(
        functools.partial(_ffn_kernel, final_norm=final_norm),
        grid=(t // tm, nf),
        in_specs=[
            pl.BlockSpec((tm, d), lambda i, j: (i, 0)),
            _mod_spec(mod_base + 0, nb, bpb, d),
            _mod_spec(mod_base + 1, nb, bpb, d),
            _mod_spec(mod_base + 2, nb, bpb, d),
            pl.BlockSpec((1, d), lambda i, j: (0, 0)),
            pl.BlockSpec((d, tf), lambda i, j: (0, j)),
            pl.BlockSpec((d, tf), lambda i, j: (0, nf + j)),
            pl.BlockSpec((tf, d), lambda i, j: (j, 0)),
            pl.BlockSpec((1, d), lambda i, j: (0, 0)),
        ],
        out_specs=pl.BlockSpec((tm, d), lambda i, j: (i, 0)),
        out_shape=jax.ShapeDtypeStruct((t, d), F32),
        scratch_shapes=[pltpu.VMEM((tm, d), BF16)],
        compiler_params=_params("arbitrary", "arbitrary"),
        name="ffn",
    )(x, mods, mods, mods, norm_g.reshape(1, d), w_in, w_in, w_out, final_g.reshape(1, d))


def _ret_in_kernel(x_ref, sh_ref, sc_ref, g_ref, pos_ref, inv_ref, w_ref, o_ref, h_ref, cos_ref, sin_ref,
                   *, n_q_tiles, dk, q_scale):
    j = pl.program_id(1)

    @pl.when(j == 0)
    def _():
        h = _modulated_norm(x_ref[...], g_ref[...], sh_ref[...], sc_ref[...])
        h_ref[...] = h.astype(BF16)
        cos, sin = _rope_tables(pos_ref, inv_ref)
        cos_ref[...] = cos
        sin_ref[...] = sin

    y = _dot(h_ref[...], w_ref[...])
    tn = y.shape[1]
    half = dk // 2

    def roped(scale):
        cos = cos_ref[...]
        sin = sin_ref[...]
        for hd in range(tn // dk):
            x1 = y[:, hd * dk:hd * dk + half]
            x2 = y[:, hd * dk + half:(hd + 1) * dk]
            o_ref[:, hd * dk:hd * dk + half] = ((x1 * cos - x2 * sin) * scale).astype(BF16)
            o_ref[:, hd * dk + half:(hd + 1) * dk] = ((x1 * sin + x2 * cos) * scale).astype(BF16)

    @pl.when(j < n_q_tiles)
    def _():
        roped(q_scale)

    @pl.when(jnp.logical_and(j >= n_q_tiles, j < 2 * n_q_tiles))
    def _():
        roped(1.0)

    @pl.when(j >= 2 * n_q_tiles)
    def _():
        o_ref[...] = y.astype(BF16)


def _ret_in(x, mods, mod_base, nb, norm_g, pos, inv, w, qk_dim, dk, tm=1024, tn=1024):
    t, d = x.shape
    n = w.shape[1]
    bpb = (t // nb) // tm
    half = dk // 2
    return pl.pallas_call(
        functools.partial(_ret_in_kernel, n_q_tiles=qk_dim // tn, dk=dk, q_scale=dk ** -0.5),
        grid=(t // tm, n // tn),
        in_specs=[
            pl.BlockSpec((tm, d), lambda i, j: (i, 0)),
            _mod_spec(mod_base + 0, nb, bpb, d),
            _mod_spec(mod_base + 1, nb, bpb, d),
            pl.BlockSpec((1, d), lambda i, j: (0, 0)),
            pl.BlockSpec((tm, 1), lambda i, j: (i, 0)),
            pl.BlockSpec((1, half), lambda i, j: (0, 0)),
            pl.BlockSpec((d, tn), lambda i, j: (0, j)),
        ],
        out_specs=pl.BlockSpec((tm, tn), lambda i, j: (i, j)),
        out_shape=jax.ShapeDtypeStruct((t, n), BF16),
        scratch_shapes=[pltpu.VMEM((tm, d), BF16), pltpu.VMEM((tm, half), F32), pltpu.VMEM((tm, half), F32)],
        compiler_params=_params("arbitrary", "arbitrary"),
        name="ret_in",
    )(x, mods, mods, norm_g.reshape(1, d), pos, inv, w)


def _ret_core_kernel(lg_ref, q_ref, k_ref, v_ref, g_ref, gn_ref, o_ref, state_ref, dmask_ref, *, span):
    i = pl.program_id(2)
    lg = lg_ref[...][:, :1]

    @pl.when(i == 0)
    def _():
        state_ref[...] = jnp.zeros_like(state_ref)
        r = lax.broadcasted_iota(jnp.int32, (span, span), 0)
        c = lax.broadcasted_iota(jnp.int32, (span, span), 1)
        decay = jnp.exp(lg * jnp.abs(r - c).astype(F32))
        dmask_ref[...] = jnp.where((c // CHUNK) <= (r // CHUNK), decay, 0.0)

    idx = lax.broadcasted_iota(jnp.int32, (span, 1), 0).astype(F32)
    xi = jnp.exp(lg * (idx + 1.0))
    zeta = jnp.exp(lg * (span - 1.0 - idx))
    g_span = jnp.exp(lg * float(span))

    q = q_ref[...]
    k = k_ref[...]
    v = v_ref[...]
    s = lax.dot_general(q, k, (((1,), (1,)), ((), ())), preferred_element_type=F32) * dmask_ref[...]
    state = state_ref[...]
    o = _dot(s.astype(BF16), v) + _dot((q.astype(F32) * xi).astype(BF16), state.astype(BF16))
    kz = (k.astype(F32) * zeta).astype(BF16)
    state_ref[...] = state * g_span + lax.dot_general(kz, v, (((0,), (0,)), ((), ())),
                                                      preferred_element_type=F32)

    mu = jnp.mean(o, axis=-1, keepdims=True)
    oc = o - mu
    var = jnp.mean(oc * oc, axis=-1, keepdims=True)
    on = (oc * lax.rsqrt(var + EPS)) * gn_ref[...]
    o_ref[...] = (_silu(g_ref[...].astype(F32)) * on).astype(BF16)


def _ret_core(qkvg, log_g, gn_g, nb, seq, heads, dk, dv, span=512):
    t = qkvg.shape[0]
    ns = seq // span
    k_off = (heads * dk) // dk
    v_off = (2 * heads * dk) // dv
    g_off = v_off + heads
    row = lambda b, h, i: b * ns + i
    return pl.pallas_call(
        functools.partial(_ret_core_kernel, span=span),
        grid=(nb, heads, ns),
        in_specs=[
            pl.BlockSpec((None, 1, 128), lambda b, h, i: (h, 0, 0)),
            pl.BlockSpec((span, dk), lambda b, h, i: (row(b, h, i), h)),
            pl.BlockSpec((span, dk), lambda b, h, i: (row(b, h, i), k_off + h)),
            pl.BlockSpec((span, dv), lambda b, h, i: (row(b, h, i), v_off + h)),
            pl.BlockSpec((span, dv), lambda b, h, i: (row(b, h, i), g_off + h)),
            pl.BlockSpec((1, dv), lambda b, h, i: (0, h)),
        ],
        out_specs=pl.BlockSpec((span, dv), lambda b, h, i: (row(b, h, i), h)),
        out_shape=jax.ShapeDtypeStruct((t, heads * dv), BF16),
        scratch_shapes=[pltpu.VMEM((dk, dv), F32), pltpu.VMEM((span, span), F32)],
        compiler_params=_params("arbitrary", "arbitrary", "arbitrary"),
        name="ret_core",
    )(log_g, qkvg, qkvg, qkvg, qkvg, gn_g.reshape(1, heads * dv))


def _proj_res_kernel(y_ref, w_ref, x_ref, gt_ref, o_ref):
    o_ref[...] = x_ref[...] + gt_ref[...] * _dot(y_ref[...], w_ref[...])


def _proj_res(y, w, x, mods, mod_idx, nb, tm=1024, tn=512):
    t, kdim = y.shape
    d = w.shape[1]
    bpb = (t // nb) // tm
    return pl.pallas_call(
        _proj_res_kernel,
        grid=(t // tm, d // tn),
        in_specs=[
            pl.BlockSpec((tm, kdim), lambda i, j: (i, 0)),
            pl.BlockSpec((kdim, tn), lambda i, j: (0, j)),
            pl.BlockSpec((tm, tn), lambda i, j: (i, j)),
            pl.BlockSpec((None, 1, tn), lambda i, j: (mod_idx * nb + i // bpb, 0, j)),
        ],
        out_specs=pl.BlockSpec((tm, tn), lambda i, j: (i, j)),
        out_shape=jax.ShapeDtypeStruct((t, d), F32),
        compiler_params=_params("arbitrary", "arbitrary"),
        name="proj_res",
    )(y, w, x, mods)


def _rope_tail(x, cos, sin):
    half = MLA_ROPE // 2
    rot = jnp.concatenate([-x[:, half:], x[:, :half]], axis=-1)
    return x * cos + rot * sin


def _kv_kernel(x_ref, sh_ref, sc_ref, g_ref, pos_ref, inv_ref, wdc_ref, wdr_ref, lg_ref, wu_ref, k_ref, v_ref):
    h = _modulated_norm(x_ref[...], g_ref[...], sh_ref[...], sc_ref[...]).astype(BF16)
    ckv = _dot(h, wdc_ref[...])
    ms = jnp.mean(ckv * ckv, axis=-1, keepdims=True)
    ckv = ((ckv * lax.rsqrt(ms + EPS)) * lg_ref[...]).astype(BF16)
    cos, sin = _rope_tables(pos_ref, inv_ref)
    kr = _rope_tail(_dot(h, wdr_ref[...]), cos, sin).astype(BF16)
    width = MLA_NOPE + MLA_V
    for hd in range(MLA_HEADS):
        kv = _dot(ckv, wu_ref[:, hd * width:(hd + 1) * width])
        k_ref[hd, :, :MLA_NOPE] = kv[:, :MLA_NOPE].astype(BF16)
        k_ref[hd, :, MLA_NOPE:] = kr
        v_ref[hd] = kv[:, MLA_NOPE:].astype(BF16)


def _shared_kv(x, mods, nb, seq, norm_g, pos, inv, w_dc, w_dr, latent_g, w_ukv, tm=512):
    t, d = x.shape
    spb = seq // tm
    const = lambda i: (0, 0)
    return pl.pallas_call(
        _kv_kernel,
        grid=(t // tm,),
        in_specs=[
            pl.BlockSpec((tm, d), lambda i: (i, 0)),
            _mod_spec(0, nb, spb, d),
            _mod_spec(1, nb, spb, d),
            pl.BlockSpec((1, d), const),
            pl.BlockSpec((tm, 1), lambda i: (i, 0)),
            pl.BlockSpec((1, MLA_ROPE), const),
            pl.BlockSpec(w_dc.shape, const),
            pl.BlockSpec(w_dr.shape, const),
            pl.BlockSpec((1, KV_RANK), const),
            pl.BlockSpec(w_ukv.shape, const),
        ],
        out_specs=[
            pl.BlockSpec((None, MLA_HEADS, tm, MLA_NOPE + MLA_ROPE), lambda i: (i // spb, 0, i % spb, 0)),
            pl.BlockSpec((None, MLA_HEADS, tm, MLA_V), lambda i: (i // spb, 0, i % spb, 0)),
        ],
        out_shape=[
            jax.ShapeDtypeStruct((nb, MLA_HEADS, seq, MLA_NOPE + MLA_ROPE), BF16),
            jax.ShapeDtypeStruct((nb, MLA_HEADS, seq, MLA_V), BF16),
        ],
        compiler_params=_params("arbitrary"),
        name="mla_kv",
    )(x, mods, mods, norm_g.reshape(1, d), pos, inv, w_dc, w_dr, latent_g.reshape(1, KV_RANK), w_ukv)


def _q_kernel(x_ref, sh_ref, sc_ref, g_ref, pos_ref, inv_ref, wd_ref, lg_ref, wu_ref, q_ref, *, q_scale):
    h = _modulated_norm(x_ref[...], g_ref[...], sh_ref[...], sc_ref[...]).astype(BF16)
    cq = _dot(h, wd_ref[...])
    ms = jnp.mean(cq * cq, axis=-1, keepdims=True)
    cq = ((cq * lax.rsqrt(ms + EPS)) * lg_ref[...]).astype(BF16)
    cos, sin = _rope_tables(pos_ref, inv_ref)
    for hd in range(MLA_HEADS):
        qh = _dot(cq, wu_ref[hd]) * q_scale
        q_ref[hd, :, :MLA_NOPE] = qh[:, :MLA_NOPE].astype(BF16)
        q_ref[hd, :, MLA_NOPE:] = _rope_tail(qh[:, MLA_NOPE:], cos, sin).astype(BF16)


def _mla_q(x, mods, mod_base, nb, seq, norm_g, pos, inv, w_dq, latent_g, w_uq_heads, tm=512):
    t, d = x.shape
    spb = seq // tm
    const = lambda i: (0, 0)
    dq = MLA_NOPE + MLA_ROPE
    return pl.pallas_call(
        functools.partial(_q_kernel, q_scale=dq ** -0.5 * math.log2(math.e)),
        grid=(t // tm,),
        in_specs=[
            pl.BlockSpec((tm, d), lambda i: (i, 0)),
            _mod_spec(mod_base + 0, nb, spb, d),
            _mod_spec(mod_base + 1, nb, spb, d),
            pl.BlockSpec((1, d), const),
            pl.BlockSpec((tm, 1), lambda i: (i, 0)),
            pl.BlockSpec((1, MLA_ROPE), const),
            pl.BlockSpec(w_dq.shape, const),
            pl.BlockSpec((1, Q_RANK), const),
            pl.BlockSpec(w_uq_heads.shape, lambda i: (0, 0, 0)),
        ],
        out_specs=pl.BlockSpec((None, MLA_HEADS, tm, dq), lambda i: (i // spb, 0, i % spb, 0)),
        out_shape=jax.ShapeDtypeStruct((nb, MLA_HEADS, seq, dq), BF16),
        compiler_params=_params("arbitrary"),
        name="mla_q",
    )(x, mods, mods, norm_g.reshape(1, d), pos, inv, w_dq, latent_g.reshape(1, Q_RANK), w_uq_heads)


def _attn_kernel(q_ref, k_ref, v_ref, mask_ref, o_ref, *, tq, nq, hg):
    qi = pl.program_id(2)
    nt = (((1,), (1,)), ((), ()))
    visible = mask_ref[...] != 0.0
    dv = v_ref.shape[-1]

    for i in range(nq):
        @pl.when(qi == i)
        def _(i=i):
            for hh in range(hg):
                q = q_ref[hh]
                m = l = acc = None
                for j in range(i, -1, -1):
                    lo = j * tq
                    s = lax.dot_general(q, k_ref[hh, lo:lo + tq, :], nt, preferred_element_type=F32)
                    if j == i:
                        s = jnp.where(visible, s, -1e30)
                    bm = jnp.max(s, axis=-1, keepdims=True)
                    m_new = bm if m is None else jnp.maximum(m, bm)
                    p = jnp.exp2(s - m_new)
                    ps = jnp.sum(p, axis=-1, keepdims=True)
                    pv = _dot(p.astype(BF16), v_ref[hh, lo:lo + tq, :])
                    if m is None:
                        l, acc = ps, pv
                    else:
                        alpha = jnp.exp2(m - m_new)
                        l = alpha * l + ps
                        acc = alpha * acc + pv
                    m = m_new
                o_ref[:, hh * dv:(hh + 1) * dv] = (acc / l).astype(BF16)


def _attention(q, k, v, tq=512, hg=2):
    nb, heads, seq, dq = q.shape
    dv = v.shape[-1]
    nq = seq // tq
    r = jnp.arange(tq, dtype=jnp.int32) // CHUNK
    mask = (r[None, :] <= r[:, None]).astype(F32)
    return pl.pallas_call(
        functools.partial(_attn_kernel, tq=tq, nq=nq, hg=hg),
        grid=(nb, heads // hg, nq),
        in_specs=[
            pl.BlockSpec((None, hg, tq, dq), lambda b, h, i: (b, h, i, 0)),
            pl.BlockSpec((None, hg, seq, dq), lambda b, h, i: (b, h, 0, 0)),
            pl.BlockSpec((None, hg, seq, dv), lambda b, h, i: (b, h, 0, 0)),
            pl.BlockSpec((tq, tq), lambda b, h, i: (0, 0)),
        ],
        out_specs=pl.BlockSpec((tq, hg * dv), lambda b, h, i: (b * nq + i, h)),
        out_shape=jax.ShapeDtypeStruct((nb * seq, heads * dv), BF16),
        compiler_params=_params("arbitrary", "arbitrary", "arbitrary"),
        name="mla_attn",
    )(q, k, v, mask)


def kernel(x, c, positions, ada_w, ada_b, norm_g, ffn_w_in, ffn_w_out, ret_w_in, ret_gn_g, ret_w_out, kv_ada_w, kv_ada_b, kv_norm_g, mla_w_dkv, kv_latent_g, mla_w_ukv, mla_w_dq, q_latent_g, mla_w_uq, mla_w_out, final_g):
    nb, seq, d = x.shape
    t = nb * seq
    depth = ada_w.shape[0]
    n_ret = ret_w_in.shape[0]
    ret_qk = d
    ret_dk = ret_qk // RET_HEADS
    ret_dv = ret_w_out.shape[1] // RET_HEADS

    xs = x.reshape(t, d)
    pos = positions.reshape(t, 1)
    c_pad = jnp.zeros((8, d), F32).at[:nb].set(c)

    half_ret = ret_dk // 2
    inv_ret = (ROPE_THETA ** (-jnp.arange(half_ret, dtype=F32) / half_ret)).reshape(1, half_ret)
    half_mla = MLA_ROPE // 2
    inv_mla_half = ROPE_THETA ** (-jnp.arange(half_mla, dtype=F32) / half_mla)
    inv_mla = jnp.concatenate([inv_mla_half, inv_mla_half]).reshape(1, MLA_ROPE)
    log_g = jnp.log1p(-(2.0 ** (-5.0 - jnp.arange(RET_HEADS, dtype=F32))))
    log_g = jnp.broadcast_to(log_g[:, None, None], (RET_HEADS, 1, 128))

    kq = kk = vv = None
    for l in range(depth):
        mods = _mod_table(_mods(c_pad, ada_w[l], ada_b[l]), nb, d)
        if l == n_ret:
            kv_mods = _mod_table(_mods(c_pad, kv_ada_w, kv_ada_b), nb, d)
            w_dkv = mla_w_dkv.astype(BF16)
            kk, vv = _shared_kv(xs, kv_mods, nb, seq, kv_norm_g, pos, inv_mla,
                                w_dkv[:, :KV_RANK], w_dkv[:, KV_RANK:], kv_latent_g, mla_w_ukv.astype(BF16))
        xs = _ffn(xs, mods, 0, nb, norm_g[l, 0], ffn_w_in[l, 0].astype(BF16), ffn_w_out[l, 0].astype(BF16),
                  final_g, False)
        if l < n_ret:
            qkvg = _ret_in(xs, mods, 3, nb, norm_g[l, 1], pos, inv_ret, ret_w_in[l].astype(BF16), ret_qk, ret_dk)
            y = _ret_core(qkvg, log_g, ret_gn_g[l], nb, seq, RET_HEADS, ret_dk, ret_dv)
            xs = _proj_res(y, ret_w_out[l].astype(BF16), xs, mods, 5, nb)
        else:
            jj = l - n_ret
            dq = MLA_NOPE + MLA_ROPE
            w_uq = mla_w_uq[jj].astype(BF16).reshape(Q_RANK, MLA_HEADS, dq).transpose(1, 0, 2)
            kq = _mla_q(xs, mods, 3, nb, seq, norm_g[l, 1], pos, inv_mla, mla_w_dq[jj].astype(BF16),
                        q_latent_g[jj], w_uq)
            y = _attention(kq, kk, vv)
            xs = _proj_res(y, mla_w_out[jj].astype(BF16), xs, mods, 5, nb)
        xs = _ffn(xs, mods, 6, nb, norm_g[l, 2], ffn_w_in[l, 1].astype(BF16), ffn_w_out[l, 1].astype(BF16),
                  final_g, l == depth - 1)
    return xs.reshape(nb, seq, d)
```

```python
import functools
import math

import jax
import jax.numpy as jnp
from jax import lax
from jax.experimental import pallas as pl
from jax.experimental.pallas import tpu as pltpu

CHUNK = 64
RET_HEADS = 8
MLA_HEADS = 16
MLA_NOPE = 128
MLA_ROPE = 64
MLA_V = 128
Q_RANK = 512
KV_RANK = 512
ROPE_THETA = 10000.0
EPS = 1e-6
N_MOD = 9

BF16 = jnp.bfloat16
F32 = jnp.float32

VMEM_LIMIT_BYTES = 56 * 1024 * 1024


def _params(*sem):
    return pltpu.CompilerParams(dimension_semantics=sem, vmem_limit_bytes=VMEM_LIMIT_BYTES)


def _silu(x):
    return x / (1.0 + jnp.exp(-x))


def _dot(a, b):
    return jnp.dot(a, b, preferred_element_type=F32)


def _modulated_norm(x, g, shift, scale):
    ms = jnp.mean(x * x, axis=-1, keepdims=True)
    return (x * lax.rsqrt(ms + EPS)) * g * (1.0 + scale) + shift


def _rope_tables(pos_ref, inv_ref):
    ang = pos_ref[...].astype(F32) * inv_ref[...]
    return jnp.cos(ang), jnp.sin(ang)


def _mod_spec(k, nb, blocks_per_batch, d):
    return pl.BlockSpec((None, 1, d), lambda i, *_: (k * nb + i // blocks_per_batch, 0, 0))


def _mods_kernel(c_ref, w_ref, b_ref, o_ref):
    ca = _silu(c_ref[...]).astype(BF16)
    o_ref[...] = _dot(ca, w_ref[...].astype(BF16)) + b_ref[...]


def _mods(c_pad, w, b, layer, tn=1024):
    rows, d = c_pad.shape
    nl, _, n = w.shape
    return pl.pallas_call(
        _mods_kernel,
        grid=(n // tn,),
        in_specs=[
            pl.BlockSpec((rows, d), lambda j: (0, 0)),
            pl.BlockSpec((None, d, tn), lambda j: (layer, 0, j)),
            pl.BlockSpec((None, 1, tn), lambda j: (layer, 0, j)),
        ],
        out_specs=pl.BlockSpec((rows, tn), lambda j: (0, j)),
        out_shape=jax.ShapeDtypeStruct((rows, n), F32),
        compiler_params=_params("arbitrary"),
        name="ada_mods",
    )(c_pad, w, b.reshape(nl, 1, n))


def _mod_table(m, nb, d):
    k = m.shape[1] // d
    return m[:nb].reshape(nb, k, d).transpose(1, 0, 2).reshape(k * nb, 1, d)


def _ffn_kernel(x_ref, sh_ref, sc_ref, gt_ref, g_ref, wg_ref, wu_ref, wo_ref, fg_ref, o_ref, h_ref,
                *, final_norm):
    j = pl.program_id(1)

    @pl.when(j == 0)
    def _():
        h = _modulated_norm(x_ref[...], g_ref[...], sh_ref[...], sc_ref[...])
        h_ref[...] = h.astype(BF16)
        o_ref[...] = jnp.zeros_like(o_ref)

    h = h_ref[...]
    gate = _dot(h, wg_ref[...])
    up = _dot(h, wu_ref[...])
    a = (_silu(gate) * up).astype(BF16)
    o_ref[...] += _dot(a, wo_ref[...])

    @pl.when(j == pl.num_programs(1) - 1)
    def _():
        y = x_ref[...] + (0.5 * gt_ref[...]) * o_ref[...]
        if final_norm:
            ms = jnp.mean(y * y, axis=-1, keepdims=True)
            y = (y * lax.rsqrt(ms + EPS)) * fg_ref[...]
        o_ref[...] = y


def _ffn(x, mods, mod_base, nb, norm_g, w_in, w_out, layer, which, final_g, final_norm, tm=1024, tf=512):
    t, d = x.shape
    dff = w_out.shape[2]
    nf = dff // tf
    bpb = (t // nb) // tm
    return pl.pallas_call(
        functools.partial(_ffn_kernel, final_norm=final_norm),
        grid=(t // tm, nf),
        in_specs=[
            pl.BlockSpec((tm, d), lambda i, j: (i, 0), pipeline_mode=pl.Buffered(1)),
            _mod_spec(mod_base + 0, nb, bpb, d),
            _mod_spec(mod_base + 1, nb, bpb, d),
            _mod_spec(mod_base + 2, nb, bpb, d),
            pl.BlockSpec((1, d), lambda i, j: (0, 0)),
            pl.BlockSpec((None, None, d, tf), lambda i, j: (layer, which, 0, j)),
            pl.BlockSpec((None, None, d, tf), lambda i, j: (layer, which, 0, nf + j)),
            pl.BlockSpec((None, None, tf, d), lambda i, j: (layer, which, j, 0)),
            pl.BlockSpec((1, d), lambda i, j: (0, 0)),
        ],
        out_specs=pl.BlockSpec((tm, d), lambda i, j: (i, 0)),
        out_shape=jax.ShapeDtypeStruct((t, d), F32),
        scratch_shapes=[pltpu.VMEM((tm, d), BF16)],
        compiler_params=_params("arbitrary", "arbitrary"),
        name="ffn",
    )(x, mods, mods, mods, norm_g.reshape(1, d), w_in, w_in, w_out, final_g.reshape(1, d))


def _ret_in_kernel(x_ref, sh_ref, sc_ref, g_ref, pos_ref, inv_ref, w_ref, o_ref, h_ref, cos_ref, sin_ref,
                   *, n_q_tiles, dk, q_scale):
    j = pl.program_id(1)

    @pl.when(j == 0)
    def _():
        h = _modulated_norm(x_ref[...], g_ref[...], sh_ref[...], sc_ref[...])
        h_ref[...] = h.astype(BF16)
        cos, sin = _rope_tables(pos_ref, inv_ref)
        cos_ref[...] = cos
        sin_ref[...] = sin

    y = _dot(h_ref[...], w_ref[...])
    tn = y.shape[1]
    half = dk // 2

    def roped(scale):
        cos = cos_ref[...]
        sin = sin_ref[...]
        for hd in range(tn // dk):
            x1 = y[:, hd * dk:hd * dk + half]
            x2 = y[:, hd * dk + half:(hd + 1) * dk]
            o_ref[:, hd * dk:hd * dk + half] = ((x1 * cos - x2 * sin) * scale).astype(BF16)
            o_ref[:, hd * dk + half:(hd + 1) * dk] = ((x1 * sin + x2 * cos) * scale).astype(BF16)

    @pl.when(j < n_q_tiles)
    def _():
        roped(q_scale)

    @pl.when(jnp.logical_and(j >= n_q_tiles, j < 2 * n_q_tiles))
    def _():
        roped(1.0)

    @pl.when(j >= 2 * n_q_tiles)
    def _():
        o_ref[...] = y.astype(BF16)


def _ret_in(x, mods, mod_base, nb, norm_g, pos, inv, w, layer, qk_dim, dk, tm=1024, tn=1024):
    t, d = x.shape
    n = w.shape[2]
    bpb = (t // nb) // tm
    half = dk // 2
    return pl.pallas_call(
        functools.partial(_ret_in_kernel, n_q_tiles=qk_dim // tn, dk=dk, q_scale=dk ** -0.5),
        grid=(t // tm, n // tn),
        in_specs=[
            pl.BlockSpec((tm, d), lambda i, j: (i, 0)),
            _mod_spec(mod_base + 0, nb, bpb, d),
            _mod_spec(mod_base + 1, nb, bpb, d),
            pl.BlockSpec((1, d), lambda i, j: (0, 0)),
            pl.BlockSpec((tm, 1), lambda i, j: (i, 0)),
            pl.BlockSpec((1, half), lambda i, j: (0, 0)),
            pl.BlockSpec((None, d, tn), lambda i, j: (layer, 0, j)),
        ],
        out_specs=pl.BlockSpec((tm, tn), lambda i, j: (i, j)),
        out_shape=jax.ShapeDtypeStruct((t, n), BF16),
        scratch_shapes=[pltpu.VMEM((tm, d), BF16), pltpu.VMEM((tm, half), F32), pltpu.VMEM((tm, half), F32)],
        compiler_params=_params("arbitrary", "arbitrary"),
        name="ret_in",
    )(x, mods, mods, norm_g.reshape(1, d), pos, inv, w)


def _ret_core_kernel(lg_ref, q_ref, k_ref, v_ref, g_ref, gn_ref, o_ref, state_ref, dmask_ref, *, span):
    i = pl.program_id(2)
    lg = lg_ref[...][:, :1]

    @pl.when(i == 0)
    def _():
        state_ref[...] = jnp.zeros_like(state_ref)
        r = lax.broadcasted_iota(jnp.int32, (span, span), 0)
        c = lax.broadcasted_iota(jnp.int32, (span, span), 1)
        decay = jnp.exp(lg * jnp.abs(r - c).astype(F32))
        dmask_ref[...] = jnp.where((c // CHUNK) <= (r // CHUNK), decay, 0.0)

    idx = lax.broadcasted_iota(jnp.int32, (span, 1), 0).astype(F32)
    xi = jnp.exp(lg * (idx + 1.0))
    zeta = jnp.exp(lg * (span - 1.0 - idx))
    g_span = jnp.exp(lg * float(span))

    q = q_ref[...]
    k = k_ref[...]
    v = v_ref[...]
    s = lax.dot_general(q, k, (((1,), (1,)), ((), ())), preferred_element_type=F32) * dmask_ref[...]
    state = state_ref[...]
    o = _dot(s.astype(BF16), v) + _dot((q.astype(F32) * xi).astype(BF16), state.astype(BF16))
    kz = (k.astype(F32) * zeta).astype(BF16)
    state_ref[...] = state * g_span + lax.dot_general(kz, v, (((0,), (0,)), ((), ())),
                                                      preferred_element_type=F32)

    mu = jnp.mean(o, axis=-1, keepdims=True)
    oc = o - mu
    var = jnp.mean(oc * oc, axis=-1, keepdims=True)
    on = (oc * lax.rsqrt(var + EPS)) * gn_ref[...]
    o_ref[...] = (_silu(g_ref[...].astype(F32)) * on).astype(BF16)


def _ret_core(qkvg, log_g, gn_g, nb, seq, heads, dk, dv, span=512):
    t = qkvg.shape[0]
    ns = seq // span
    k_off = (heads * dk) // dk
    v_off = (2 * heads * dk) // dv
    g_off = v_off + heads
    row = lambda b, h, i: b * ns + i
    return pl.pallas_call(
        functools.partial(_ret_core_kernel, span=span),
        grid=(nb, heads, ns),
        in_specs=[
            pl.BlockSpec((None, 1, 128), lambda b, h, i: (h, 0, 0)),
            pl.BlockSpec((span, dk), lambda b, h, i: (row(b, h, i), h)),
            pl.BlockSpec((span, dk), lambda b, h, i: (row(b, h, i), k_off + h)),
            pl.BlockSpec((span, dv), lambda b, h, i: (row(b, h, i), v_off + h)),
            pl.BlockSpec((span, dv), lambda b, h, i: (row(b, h, i), g_off + h)),
            pl.BlockSpec((1, dv), lambda b, h, i: (0, h)),
        ],
        out_specs=pl.BlockSpec((span, dv), lambda b, h, i: (row(b, h, i), h)),
        out_shape=jax.ShapeDtypeStruct((t, heads * dv), BF16),
        scratch_shapes=[pltpu.VMEM((dk, dv), F32), pltpu.VMEM((span, span), F32)],
        compiler_params=_params("arbitrary", "arbitrary", "arbitrary"),
        name="ret_core",
    )(log_g, qkvg, qkvg, qkvg, qkvg, gn_g.reshape(1, heads * dv))


def _proj_res_kernel(y_ref, w_ref, x_ref, gt_ref, o_ref):
    o_ref[...] = x_ref[...] + gt_ref[...] * _dot(y_ref[...], w_ref[...])


def _proj_res(y, w, layer, x, mods, mod_idx, nb, tm=1024, tn=512):
    t, kdim = y.shape
    d = w.shape[2]
    bpb = (t // nb) // tm
    return pl.pallas_call(
        _proj_res_kernel,
        grid=(t // tm, d // tn),
        in_specs=[
            pl.BlockSpec((tm, kdim), lambda i, j: (i, 0)),
            pl.BlockSpec((None, kdim, tn), lambda i, j: (layer, 0, j)),
            pl.BlockSpec((tm, tn), lambda i, j: (i, j)),
            pl.BlockSpec((None, 1, tn), lambda i, j: (mod_idx * nb + i // bpb, 0, j)),
        ],
        out_specs=pl.BlockSpec((tm, tn), lambda i, j: (i, j)),
        out_shape=jax.ShapeDtypeStruct((t, d), F32),
        compiler_params=_params("arbitrary", "arbitrary"),
        name="proj_res",
    )(y, w, x, mods)


def _rope_table_kernel(pos_ref, inv_ref, cos_ref, sin_ref):
    cos, sin = _rope_tables(pos_ref, inv_ref)
    cos_ref[...] = cos
    sin_ref[...] = sin


def _mla_rope_tables(positions, half, tr=512):
    t = positions.size
    per_row = 128 // half
    rows = t // per_row
    pos_rep = jnp.repeat(positions.reshape(rows, per_row), half, axis=1)
    inv = ROPE_THETA ** (-jnp.arange(half, dtype=F32) / half)
    inv_rep = jnp.tile(inv, per_row).reshape(1, 128)
    cos, sin = pl.pallas_call(
        _rope_table_kernel,
        grid=(rows // tr,),
        in_specs=[pl.BlockSpec((tr, 128), lambda i: (i, 0)), pl.BlockSpec((1, 128), lambda i: (0, 0))],
        out_specs=[pl.BlockSpec((tr, 128), lambda i: (i, 0))] * 2,
        out_shape=[jax.ShapeDtypeStruct((rows, 128), F32)] * 2,
        compiler_params=_params("arbitrary"),
        name="mla_rope_tables",
    )(pos_rep, inv_rep)
    return cos.reshape(t, half), sin.reshape(t, half)


def _rope_tail(x, cos, sin):
    half = MLA_ROPE // 2
    x1 = x[:, :half]
    x2 = x[:, half:]
    return jnp.concatenate([x1 * cos - x2 * sin, x1 * sin + x2 * cos], axis=-1)


def _kv_kernel(x_ref, sh_ref, sc_ref, g_ref, cos_ref, sin_ref, wdc_ref, wdr_ref, lg_ref, wu_ref, k_ref, v_ref):
    h = _modulated_norm(x_ref[...], g_ref[...], sh_ref[...], sc_ref[...]).astype(BF16)
    ckv = _dot(h, wdc_ref[...])
    ms = jnp.mean(ckv * ckv, axis=-1, keepdims=True)
    ckv = ((ckv * lax.rsqrt(ms + EPS)) * lg_ref[...]).astype(BF16)
    kr = _rope_tail(_dot(h, wdr_ref[...]), cos_ref[...], sin_ref[...]).astype(BF16)
    width = MLA_NOPE + MLA_V
    for hd in range(MLA_HEADS):
        kv = _dot(ckv, wu_ref[:, hd * width:(hd + 1) * width])
        k_ref[hd, :, :MLA_NOPE] = kv[:, :MLA_NOPE].astype(BF16)
        k_ref[hd, :, MLA_NOPE:] = kr
        v_ref[hd] = kv[:, MLA_NOPE:].astype(BF16)


def _shared_kv(x, mods, nb, seq, norm_g, cos, sin, w_dc, w_dr, latent_g, w_ukv, tm=512):
    t, d = x.shape
    spb = seq // tm
    const = lambda i: (0, 0)
    return pl.pallas_call(
        _kv_kernel,
        grid=(t // tm,),
        in_specs=[
            pl.BlockSpec((tm, d), lambda i: (i, 0)),
            _mod_spec(0, nb, spb, d),
            _mod_spec(1, nb, spb, d),
            pl.BlockSpec((1, d), const),
            pl.BlockSpec((tm, MLA_ROPE // 2), lambda i: (i, 0)),
            pl.BlockSpec((tm, MLA_ROPE // 2), lambda i: (i, 0)),
            pl.BlockSpec(w_dc.shape, const),
            pl.BlockSpec(w_dr.shape, const),
            pl.BlockSpec((1, KV_RANK), const),
            pl.BlockSpec(w_ukv.shape, const),
        ],
        out_specs=[
            pl.BlockSpec((None, MLA_HEADS, tm, MLA_NOPE + MLA_ROPE), lambda i: (i // spb, 0, i % spb, 0)),
            pl.BlockSpec((None, MLA_HEADS, tm, MLA_V), lambda i: (i // spb, 0, i % spb, 0)),
        ],
        out_shape=[
            jax.ShapeDtypeStruct((nb, MLA_HEADS, seq, MLA_NOPE + MLA_ROPE), BF16),
            jax.ShapeDtypeStruct((nb, MLA_HEADS, seq, MLA_V), BF16),
        ],
        compiler_params=_params("arbitrary"),
        name="mla_kv",
    )(x, mods, mods, norm_g.reshape(1, d), cos, sin, w_dc, w_dr, latent_g.reshape(1, KV_RANK), w_ukv)


def _q_kernel(x_ref, sh_ref, sc_ref, g_ref, cos_ref, sin_ref, wd_ref, lg_ref, wu_ref, q_ref, *, q_scale):
    h = _modulated_norm(x_ref[...], g_ref[...], sh_ref[...], sc_ref[...]).astype(BF16)
    cq = _dot(h, wd_ref[...])
    ms = jnp.mean(cq * cq, axis=-1, keepdims=True)
    cq = ((cq * lax.rsqrt(ms + EPS)) * lg_ref[...]).astype(BF16)
    cos = cos_ref[...]
    sin = sin_ref[...]
    for hd in range(MLA_HEADS):
        qh = _dot(cq, wu_ref[hd]) * q_scale
        q_ref[hd, :, :MLA_NOPE] = qh[:, :MLA_NOPE].astype(BF16)
        q_ref[hd, :, MLA_NOPE:] = _rope_tail(qh[:, MLA_NOPE:], cos, sin).astype(BF16)


def _mla_q(x, mods, mod_base, nb, seq, norm_g, cos, sin, w_dq, latent_g, w_uq_heads, tm=512):
    t, d = x.shape
    spb = seq // tm
    const = lambda i: (0, 0)
    dq = MLA_NOPE + MLA_ROPE
    return pl.pallas_call(
        functools.partial(_q_kernel, q_scale=dq ** -0.5 * math.log2(math.e)),
        grid=(t // tm,),
        in_specs=[
            pl.BlockSpec((tm, d), lambda i: (i, 0)),
            _mod_spec(mod_base + 0, nb, spb, d),
            _mod_spec(mod_base + 1, nb, spb, d),
            pl.BlockSpec((1, d), const),
            pl.BlockSpec((tm, MLA_ROPE // 2), lambda i: (i, 0)),
            pl.BlockSpec((tm, MLA_ROPE // 2), lambda i: (i, 0)),
            pl.BlockSpec(w_dq.shape, const),
            pl.BlockSpec((1, Q_RANK), const),
            pl.BlockSpec(w_uq_heads.shape, lambda i: (0, 0, 0)),
        ],
        out_specs=pl.BlockSpec((None, MLA_HEADS, tm, dq), lambda i: (i // spb, 0, i % spb, 0)),
        out_shape=jax.ShapeDtypeStruct((nb, MLA_HEADS, seq, dq), BF16),
        compiler_params=_params("arbitrary"),
        name="mla_q",
    )(x, mods, mods, norm_g.reshape(1, d), cos, sin, w_dq, latent_g.reshape(1, Q_RANK), w_uq_heads)


def _attn_kernel(q_ref, k_ref, v_ref, mask_ref, o_ref, *, tq, nq, hg):
    qi = pl.program_id(2)
    nt = (((1,), (1,)), ((), ()))
    visible = mask_ref[...] != 0.0
    dv = v_ref.shape[-1]

    for i in range(nq):
        @pl.when(qi == i)
        def _(i=i):
            for hh in range(hg):
                q = q_ref[hh]
                m = l = acc = None
                for j in range(i, -1, -1):
                    lo = j * tq
                    s = lax.dot_general(q, k_ref[hh, lo:lo + tq, :], nt, preferred_element_type=F32)
                    if j == i:
                        s = jnp.where(visible, s, -1e30)
                    bm = jnp.max(s, axis=-1, keepdims=True)
                    m_new = bm if m is None else jnp.maximum(m, bm)
                    p = jnp.exp2(s - m_new)
                    ps = jnp.sum(p, axis=-1, keepdims=True)
                    pv = _dot(p.astype(BF16), v_ref[hh, lo:lo + tq, :])
                    if m is None:
                        l, acc = ps, pv
                    else:
                        alpha = jnp.exp2(m - m_new)
                        l = alpha * l + ps
                        acc = alpha * acc + pv
                    m = m_new
                o_ref[:, hh * dv:(hh + 1) * dv] = (acc / l).astype(BF16)


def _attention(q, k, v, tq=512, hg=2):
    nb, heads, seq, dq = q.shape
    dv = v.shape[-1]
    nq = seq // tq
    r = jnp.arange(tq, dtype=jnp.int32) // CHUNK
    mask = (r[None, :] <= r[:, None]).astype(F32)
    return pl.pallas_call(
        functools.partial(_attn_kernel, tq=tq, nq=nq, hg=hg),
        grid=(nb, heads // hg, nq),
        in_specs=[
            pl.BlockSpec((None, hg, tq, dq), lambda b, h, i: (b, h, i, 0)),
            pl.BlockSpec((None, hg, seq, dq), lambda b, h, i: (b, h, 0, 0)),
            pl.BlockSpec((None, hg, seq, dv), lambda b, h, i: (b, h, 0, 0)),
            pl.BlockSpec((tq, tq), lambda b, h, i: (0, 0)),
        ],
        out_specs=pl.BlockSpec((tq, hg * dv), lambda b, h, i: (b * nq + i, h)),
        out_shape=jax.ShapeDtypeStruct((nb * seq, heads * dv), BF16),
        compiler_params=_params("arbitrary", "arbitrary", "arbitrary"),
        name="mla_attn",
    )(q, k, v, mask)


def kernel(x, c, positions, ada_w, ada_b, norm_g, ffn_w_in, ffn_w_out, ret_w_in, ret_gn_g, ret_w_out, kv_ada_w, kv_ada_b, kv_norm_g, mla_w_dkv, kv_latent_g, mla_w_ukv, mla_w_dq, q_latent_g, mla_w_uq, mla_w_out, final_g):
    nb, seq, d = x.shape
    t = nb * seq
    depth = ada_w.shape[0]
    n_ret = ret_w_in.shape[0]
    ret_qk = d
    ret_dk = ret_qk // RET_HEADS
    ret_dv = ret_w_out.shape[1] // RET_HEADS

    xs = x.reshape(t, d)
    pos = positions.reshape(t, 1)
    c_pad = jnp.zeros((8, d), F32).at[:nb].set(c)

    half_ret = ret_dk // 2
    inv_ret = (ROPE_THETA ** (-jnp.arange(half_ret, dtype=F32) / half_ret)).reshape(1, half_ret)
    mla_cos, mla_sin = _mla_rope_tables(positions, MLA_ROPE // 2)
    log_g = jnp.log1p(-(2.0 ** (-5.0 - jnp.arange(RET_HEADS, dtype=F32))))
    log_g = jnp.broadcast_to(log_g[:, None, None], (RET_HEADS, 1, 128))

    ffn_in = ffn_w_in.astype(BF16)
    ffn_out = ffn_w_out.astype(BF16)
    ret_in_w = ret_w_in.astype(BF16)
    ret_out_w = ret_w_out.astype(BF16)
    mla_out_w = mla_w_out.astype(BF16)

    kq = kk = vv = None
    for l in range(depth):
        mods = _mod_table(_mods(c_pad, ada_w, ada_b, l), nb, d)
        if l == n_ret:
            kv_mods = _mod_table(_mods(c_pad, kv_ada_w[None], kv_ada_b[None], 0), nb, d)
            w_dkv = mla_w_dkv.astype(BF16)
            kk, vv = _shared_kv(xs, kv_mods, nb, seq, kv_norm_g, mla_cos, mla_sin,
                                w_dkv[:, :KV_RANK], w_dkv[:, KV_RANK:], kv_latent_g, mla_w_ukv.astype(BF16))
        xs = _ffn(xs, mods, 0, nb, norm_g[l, 0], ffn_in, ffn_out, l, 0, final_g, False)
        if l < n_ret:
            qkvg = _ret_in(xs, mods, 3, nb, norm_g[l, 1], pos, inv_ret, ret_in_w, l, ret_qk, ret_dk)
            y = _ret_core(qkvg, log_g, ret_gn_g[l], nb, seq, RET_HEADS, ret_dk, ret_dv)
            xs = _proj_res(y, ret_out_w, l, xs, mods, 5, nb)
        else:
            jj = l - n_ret
            dq = MLA_NOPE + MLA_ROPE
            w_uq = mla_w_uq[jj].astype(BF16).reshape(Q_RANK, MLA_HEADS, dq).transpose(1, 0, 2)
            kq = _mla_q(xs, mods, 3, nb, seq, norm_g[l, 1], mla_cos, mla_sin, mla_w_dq[jj].astype(BF16),
                        q_latent_g[jj], w_uq)
            y = _attention(kq, kk, vv)
            xs = _proj_res(y, mla_out_w, jj, xs, mods, 5, nb)
        xs = _ffn(xs, mods, 6, nb, norm_g[l, 2], ffn_in, ffn_out, l, 1, final_g, l == depth - 1)
    return xs.reshape(nb, seq, d)
```

```python
import functools
import math

import jax
import jax.numpy as jnp
from jax import lax
from jax.experimental import pallas as pl
from jax.experimental.pallas import tpu as pltpu

CHUNK = 64
RET_HEADS = 8
MLA_HEADS = 16
MLA_NOPE = 128
MLA_ROPE = 64
MLA_V = 128
Q_RANK = 512
KV_RANK = 512
ROPE_THETA = 10000.0
EPS = 1e-6
N_MOD = 9

BF16 = jnp.bfloat16
F32 = jnp.float32

VMEM_LIMIT_BYTES = 56 * 1024 * 1024


def _params(*sem):
    return pltpu.CompilerParams(dimension_semantics=sem, vmem_limit_bytes=VMEM_LIMIT_BYTES)


def _silu(x):
    return x / (1.0 + jnp.exp(-x))


def _dot(a, b):
    return jnp.dot(a, b, preferred_element_type=F32)


def _modulated_norm(x, g, shift, scale):
    ms = jnp.mean(x * x, axis=-1, keepdims=True)
    return (x * lax.rsqrt(ms + EPS)) * g * (1.0 + scale) + shift


def _rope_tables(pos_ref, inv_ref):
    ang = pos_ref[...].astype(F32) * inv_ref[...]
    return jnp.cos(ang), jnp.sin(ang)


def _mod_spec(k, nb, blocks_per_batch, d):
    return pl.BlockSpec((None, 1, d), lambda i, *_: (k * nb + i // blocks_per_batch, 0, 0))


def _mods_kernel(c_ref, w_ref, b_ref, o_ref):
    ca = _silu(c_ref[...]).astype(BF16)
    o_ref[...] = _dot(ca, w_ref[...].astype(BF16)) + b_ref[...]


def _mods(c_pad, w, b, layer, tn=1024):
    rows, d = c_pad.shape
    nl, _, n = w.shape
    return pl.pallas_call(
        _mods_kernel,
        grid=(n // tn,),
        in_specs=[
            pl.BlockSpec((rows, d), lambda j: (0, 0)),
            pl.BlockSpec((None, d, tn), lambda j: (layer, 0, j)),
            pl.BlockSpec((None, 1, tn), lambda j: (layer, 0, j)),
        ],
        out_specs=pl.BlockSpec((rows, tn), lambda j: (0, j)),
        out_shape=jax.ShapeDtypeStruct((rows, n), F32),
        compiler_params=_params("arbitrary"),
        name="ada_mods",
    )(c_pad, w, b.reshape(nl, 1, n))


def _mod_table(m, nb, d):
    k = m.shape[1] // d
    return m[:nb].reshape(nb, k, d).transpose(1, 0, 2).reshape(k * nb, 1, d)


def _ffn_kernel(x_ref, sh_ref, sc_ref, gt_ref, g_ref, wg_ref, wu_ref, wo_ref, fg_ref, o_ref, h_ref,
                *, final_norm, row_splits):
    j = pl.program_id(1)
    rs = x_ref.shape[0] // row_splits
    slabs = [slice(r * rs, (r + 1) * rs) for r in range(row_splits)]

    def per_slab(fn):
        def body(r, carry):
            fn(pl.ds(pl.multiple_of(r * rs, rs), rs))
            return carry
        lax.fori_loop(0, row_splits, body, 0)

    @pl.when(j == 0)
    def _():
        def prologue(rows):
            h = _modulated_norm(x_ref[rows, :], g_ref[...], sh_ref[...], sc_ref[...])
            h_ref[rows, :] = h.astype(BF16)
            o_ref[rows, :] = jnp.zeros((rs, o_ref.shape[1]), F32)
        per_slab(prologue)

    for rows in slabs:
        h = h_ref[rows, :]
        gate = _dot(h, wg_ref[...])
        up = _dot(h, wu_ref[...])
        a = (_silu(gate) * up).astype(BF16)
        o_ref[rows, :] += _dot(a, wo_ref[...])

    @pl.when(j == pl.num_programs(1) - 1)
    def _():
        def epilogue(rows):
            y = x_ref[rows, :] + (0.5 * gt_ref[...]) * o_ref[rows, :]
            if final_norm:
                ms = jnp.mean(y * y, axis=-1, keepdims=True)
                y = (y * lax.rsqrt(ms + EPS)) * fg_ref[...]
            o_ref[rows, :] = y
        per_slab(epilogue)


def _ffn(x, mods, mod_base, nb, norm_g, w_in, w_out, layer, which, final_g, final_norm, tm=1024, tf=512):
    t, d = x.shape
    dff = w_out.shape[2]
    nf = dff // tf
    bpb = (t // nb) // tm
    return pl.pallas_call(
        functools.partial(_ffn_kernel, final_norm=final_norm, row_splits=4),
        grid=(t // tm, nf),
        in_specs=[
            pl.BlockSpec((tm, d), lambda i, j: (i, 0)),
            _mod_spec(mod_base + 0, nb, bpb, d),
            _mod_spec(mod_base + 1, nb, bpb, d),
            _mod_spec(mod_base + 2, nb, bpb, d),
            pl.BlockSpec((1, d), lambda i, j: (0, 0)),
            pl.BlockSpec((None, None, d, tf), lambda i, j: (layer, which, 0, j)),
            pl.BlockSpec((None, None, d, tf), lambda i, j: (layer, which, 0, nf + j)),
            pl.BlockSpec((None, None, tf, d), lambda i, j: (layer, which, j, 0)),
            pl.BlockSpec((1, d), lambda i, j: (0, 0)),
        ],
        out_specs=pl.BlockSpec((tm, d), lambda i, j: (i, 0)),
        out_shape=jax.ShapeDtypeStruct((t, d), F32),
        scratch_shapes=[pltpu.VMEM((tm, d), BF16)],
        compiler_params=_params("arbitrary", "arbitrary"),
        name="ffn",
    )(x, mods, mods, mods, norm_g.reshape(1, d), w_in, w_in, w_out, final_g.reshape(1, d))


def _ret_in_kernel(x_ref, sh_ref, sc_ref, g_ref, pos_ref, inv_ref, w_ref, o_ref, h_ref, cos_ref, sin_ref,
                   *, n_q_tiles, dk, q_scale):
    j = pl.program_id(1)

    @pl.when(j == 0)
    def _():
        h = _modulated_norm(x_ref[...], g_ref[...], sh_ref[...], sc_ref[...])
        h_ref[...] = h.astype(BF16)
        cos, sin = _rope_tables(pos_ref, inv_ref)
        cos_ref[...] = cos
        sin_ref[...] = sin

    y = _dot(h_ref[...], w_ref[...])
    tn = y.shape[1]
    half = dk // 2

    def roped(scale):
        cos = cos_ref[...]
        sin = sin_ref[...]
        for hd in range(tn // dk):
            x1 = y[:, hd * dk:hd * dk + half]
            x2 = y[:, hd * dk + half:(hd + 1) * dk]
            o_ref[:, hd * dk:hd * dk + half] = ((x1 * cos - x2 * sin) * scale).astype(BF16)
            o_ref[:, hd * dk + half:(hd + 1) * dk] = ((x1 * sin + x2 * cos) * scale).astype(BF16)

    @pl.when(j < n_q_tiles)
    def _():
        roped(q_scale)

    @pl.when(jnp.logical_and(j >= n_q_tiles, j < 2 * n_q_tiles))
    def _():
        roped(1.0)

    @pl.when(j >= 2 * n_q_tiles)
    def _():
        o_ref[...] = y.astype(BF16)


def _ret_in(x, mods, mod_base, nb, norm_g, pos, inv, w, layer, qk_dim, dk, tm=1024, tn=1024):
    t, d = x.shape
    n = w.shape[2]
    bpb = (t // nb) // tm
    half = dk // 2
    return pl.pallas_call(
        functools.partial(_ret_in_kernel, n_q_tiles=qk_dim // tn, dk=dk, q_scale=dk ** -0.5),
        grid=(t // tm, n // tn),
        in_specs=[
            pl.BlockSpec((tm, d), lambda i, j: (i, 0)),
            _mod_spec(mod_base + 0, nb, bpb, d),
            _mod_spec(mod_base + 1, nb, bpb, d),
            pl.BlockSpec((1, d), lambda i, j: (0, 0)),
            pl.BlockSpec((tm, 1), lambda i, j: (i, 0)),
            pl.BlockSpec((1, half), lambda i, j: (0, 0)),
            pl.BlockSpec((None, d, tn), lambda i, j: (layer, 0, j)),
        ],
        out_specs=pl.BlockSpec((tm, tn), lambda i, j: (i, j)),
        out_shape=jax.ShapeDtypeStruct((t, n), BF16),
        scratch_shapes=[pltpu.VMEM((tm, d), BF16), pltpu.VMEM((tm, half), F32), pltpu.VMEM((tm, half), F32)],
        compiler_params=_params("arbitrary", "arbitrary"),
        name="ret_in",
    )(x, mods, mods, norm_g.reshape(1, d), pos, inv, w)


def _ret_core_kernel(lg_ref, q_ref, k_ref, v_ref, g_ref, gn_ref, o_ref, state_ref, dmask_ref, *, span, hg, dk, dv):
    i = pl.program_id(2)

    @pl.when(i == 0)
    def _():
        state_ref[...] = jnp.zeros_like(state_ref)
        r = lax.broadcasted_iota(jnp.int32, (span, span), 0)
        c = lax.broadcasted_iota(jnp.int32, (span, span), 1)
        dist = jnp.abs(r - c).astype(F32)
        causal = (c // CHUNK) <= (r // CHUNK)
        for hh in range(hg):
            dmask_ref[hh] = jnp.where(causal, jnp.exp(lg_ref[hh][:, :1] * dist), 0.0)

    idx = lax.broadcasted_iota(jnp.int32, (span, 1), 0).astype(F32)
    for hh in range(hg):
        lg = lg_ref[hh][:, :1]
        xi = jnp.exp(lg * (idx + 1.0))
        zeta = jnp.exp(lg * (span - 1.0 - idx))
        g_span = jnp.exp(lg * float(span))

        q = q_ref[:, hh * dk:(hh + 1) * dk]
        k = k_ref[:, hh * dk:(hh + 1) * dk]
        v = v_ref[:, hh * dv:(hh + 1) * dv]
        s = lax.dot_general(q, k, (((1,), (1,)), ((), ())), preferred_element_type=F32) * dmask_ref[hh]
        state = state_ref[hh]
        o = _dot(s.astype(BF16), v) + _dot((q.astype(F32) * xi).astype(BF16), state.astype(BF16))
        kz = (k.astype(F32) * zeta).astype(BF16)
        state_ref[hh] = state * g_span + lax.dot_general(kz, v, (((0,), (0,)), ((), ())),
                                                         preferred_element_type=F32)

        mu = jnp.mean(o, axis=-1, keepdims=True)
        oc = o - mu
        var = jnp.mean(oc * oc, axis=-1, keepdims=True)
        on = (oc * lax.rsqrt(var + EPS)) * gn_ref[:, hh * dv:(hh + 1) * dv]
        gate = _silu(g_ref[:, hh * dv:(hh + 1) * dv].astype(F32))
        o_ref[:, hh * dv:(hh + 1) * dv] = (gate * on).astype(BF16)


def _ret_core(qkvg, log_g, gn_g, nb, seq, heads, dk, dv, span=512, hg=2):
    t = qkvg.shape[0]
    ns = seq // span
    groups = heads // hg
    k_off = groups
    v_off = (2 * heads * dk) // (hg * dv)
    g_off = v_off + groups
    row = lambda b, h, i: b * ns + i
    return pl.pallas_call(
        functools.partial(_ret_core_kernel, span=span, hg=hg, dk=dk, dv=dv),
        grid=(nb, groups, ns),
        in_specs=[
            pl.BlockSpec((hg, 1, 128), lambda b, h, i: (h, 0, 0)),
            pl.BlockSpec((span, hg * dk), lambda b, h, i: (row(b, h, i), h)),
            pl.BlockSpec((span, hg * dk), lambda b, h, i: (row(b, h, i), k_off + h)),
            pl.BlockSpec((span, hg * dv), lambda b, h, i: (row(b, h, i), v_off + h)),
            pl.BlockSpec((span, hg * dv), lambda b, h, i: (row(b, h, i), g_off + h)),
            pl.BlockSpec((1, hg * dv), lambda b, h, i: (0, h)),
        ],
        out_specs=pl.BlockSpec((span, hg * dv), lambda b, h, i: (row(b, h, i), h)),
        out_shape=jax.ShapeDtypeStruct((t, heads * dv), BF16),
        scratch_shapes=[pltpu.VMEM((hg, dk, dv), F32), pltpu.VMEM((hg, span, span), F32)],
        compiler_params=_params("arbitrary", "arbitrary", "arbitrary"),
        name="ret_core",
    )(log_g, qkvg, qkvg, qkvg, qkvg, gn_g.reshape(1, heads * dv))


def _proj_res_kernel(y_ref, w_ref, x_ref, gt_ref, o_ref):
    o_ref[...] = x_ref[...] + gt_ref[...] * _dot(y_ref[...], w_ref[...])


def _proj_res(y, w, layer, x, mods, mod_idx, nb, tm=1024, tn=512):
    t, kdim = y.shape
    d = w.shape[2]
    bpb = (t // nb) // tm
    return pl.pallas_call(
        _proj_res_kernel,
        grid=(t // tm, d // tn),
        in_specs=[
            pl.BlockSpec((tm, kdim), lambda i, j: (i, 0)),
            pl.BlockSpec((None, kdim, tn), lambda i, j: (layer, 0, j)),
            pl.BlockSpec((tm, tn), lambda i, j: (i, j)),
            pl.BlockSpec((None, 1, tn), lambda i, j: (mod_idx * nb + i // bpb, 0, j)),
        ],
        out_specs=pl.BlockSpec((tm, tn), lambda i, j: (i, j)),
        out_shape=jax.ShapeDtypeStruct((t, d), F32),
        compiler_params=_params("arbitrary", "arbitrary"),
        name="proj_res",
    )(y, w, x, mods)


def _rope_table_kernel(pos_ref, inv_ref, cos_ref, sin_ref):
    cos, sin = _rope_tables(pos_ref, inv_ref)
    cos_ref[...] = cos
    sin_ref[...] = sin


def _mla_rope_tables(positions, half, tr=512):
    t = positions.size
    per_row = 128 // half
    rows = t // per_row
    pos_rep = jnp.repeat(positions.reshape(rows, per_row), half, axis=1)
    inv = ROPE_THETA ** (-jnp.arange(half, dtype=F32) / half)
    inv_rep = jnp.tile(inv, per_row).reshape(1, 128)
    cos, sin = pl.pallas_call(
        _rope_table_kernel,
        grid=(rows // tr,),
        in_specs=[pl.BlockSpec((tr, 128), lambda i: (i, 0)), pl.BlockSpec((1, 128), lambda i: (0, 0))],
        out_specs=[pl.BlockSpec((tr, 128), lambda i: (i, 0))] * 2,
        out_shape=[jax.ShapeDtypeStruct((rows, 128), F32)] * 2,
        compiler_params=_params("arbitrary"),
        name="mla_rope_tables",
    )(pos_rep, inv_rep)
    return cos.reshape(t, half), sin.reshape(t, half)


def _rope_tail(x, cos, sin):
    half = MLA_ROPE // 2
    x1 = x[:, :half]
    x2 = x[:, half:]
    return jnp.concatenate([x1 * cos - x2 * sin, x1 * sin + x2 * cos], axis=-1)


def _kv_kernel(x_ref, sh_ref, sc_ref, g_ref, cos_ref, sin_ref, wdc_ref, wdr_ref, lg_ref, wu_ref, k_ref, v_ref):
    h = _modulated_norm(x_ref[...], g_ref[...], sh_ref[...], sc_ref[...]).astype(BF16)
    ckv = _dot(h, wdc_ref[...])
    ms = jnp.mean(ckv * ckv, axis=-1, keepdims=True)
    ckv = ((ckv * lax.rsqrt(ms + EPS)) * lg_ref[...]).astype(BF16)
    kr = _rope_tail(_dot(h, wdr_ref[...]), cos_ref[...], sin_ref[...]).astype(BF16)
    width = MLA_NOPE + MLA_V
    for hd in range(MLA_HEADS):
        kv = _dot(ckv, wu_ref[:, hd * width:(hd + 1) * width])
        k_ref[hd, :, :MLA_NOPE] = kv[:, :MLA_NOPE].astype(BF16)
        k_ref[hd, :, MLA_NOPE:] = kr
        v_ref[hd] = kv[:, MLA_NOPE:].astype(BF16)


def _shared_kv(x, mods, nb, seq, norm_g, cos, sin, w_dc, w_dr, latent_g, w_ukv, tm=512):
    t, d = x.shape
    spb = seq // tm
    const = lambda i: (0, 0)
    return pl.pallas_call(
        _kv_kernel,
        grid=(t // tm,),
        in_specs=[
            pl.BlockSpec((tm, d), lambda i: (i, 0)),
            _mod_spec(0, nb, spb, d),
            _mod_spec(1, nb, spb, d),
            pl.BlockSpec((1, d), const),
            pl.BlockSpec((tm, MLA_ROPE // 2), lambda i: (i, 0)),
            pl.BlockSpec((tm, MLA_ROPE // 2), lambda i: (i, 0)),
            pl.BlockSpec(w_dc.shape, const),
            pl.BlockSpec(w_dr.shape, const),
            pl.BlockSpec((1, KV_RANK), const),
            pl.BlockSpec(w_ukv.shape, const),
        ],
        out_specs=[
            pl.BlockSpec((None, MLA_HEADS, tm, MLA_NOPE + MLA_ROPE), lambda i: (i // spb, 0, i % spb, 0)),
            pl.BlockSpec((None, MLA_HEADS, tm, MLA_V), lambda i: (i // spb, 0, i % spb, 0)),
        ],
        out_shape=[
            jax.ShapeDtypeStruct((nb, MLA_HEADS, seq, MLA_NOPE + MLA_ROPE), BF16),
            jax.ShapeDtypeStruct((nb, MLA_HEADS, seq, MLA_V), BF16),
        ],
        compiler_params=_params("arbitrary"),
        name="mla_kv",
    )(x, mods, mods, norm_g.reshape(1, d), cos, sin, w_dc, w_dr, latent_g.reshape(1, KV_RANK), w_ukv)


def _q_kernel(x_ref, sh_ref, sc_ref, g_ref, cos_ref, sin_ref, wd_ref, lg_ref, wu_ref, q_ref, *, q_scale):
    h = _modulated_norm(x_ref[...], g_ref[...], sh_ref[...], sc_ref[...]).astype(BF16)
    cq = _dot(h, wd_ref[...])
    ms = jnp.mean(cq * cq, axis=-1, keepdims=True)
    cq = ((cq * lax.rsqrt(ms + EPS)) * lg_ref[...]).astype(BF16)
    cos = cos_ref[...]
    sin = sin_ref[...]
    for hd in range(MLA_HEADS):
        qh = _dot(cq, wu_ref[hd]) * q_scale
        q_ref[hd, :, :MLA_NOPE] = qh[:, :MLA_NOPE].astype(BF16)
        q_ref[hd, :, MLA_NOPE:] = _rope_tail(qh[:, MLA_NOPE:], cos, sin).astype(BF16)


def _mla_q(x, mods, mod_base, nb, seq, norm_g, cos, sin, w_dq, latent_g, w_uq_heads, tm=512):
    t, d = x.shape
    spb = seq // tm
    const = lambda i: (0, 0)
    dq = MLA_NOPE + MLA_ROPE
    return pl.pallas_call(
        functools.partial(_q_kernel, q_scale=dq ** -0.5 * math.log2(math.e)),
        grid=(t // tm,),
        in_specs=[
            pl.BlockSpec((tm, d), lambda i: (i, 0)),
            _mod_spec(mod_base + 0, nb, spb, d),
            _mod_spec(mod_base + 1, nb, spb, d),
            pl.BlockSpec((1, d), const),
            pl.BlockSpec((tm, MLA_ROPE // 2), lambda i: (i, 0)),
            pl.BlockSpec((tm, MLA_ROPE // 2), lambda i: (i, 0)),
            pl.BlockSpec(w_dq.shape, const),
            pl.BlockSpec((1, Q_RANK), const),
            pl.BlockSpec(w_uq_heads.shape, lambda i: (0, 0, 0)),
        ],
        out_specs=pl.BlockSpec((None, MLA_HEADS, tm, dq), lambda i: (i // spb, 0, i % spb, 0)),
        out_shape=jax.ShapeDtypeStruct((nb, MLA_HEADS, seq, dq), BF16),
        compiler_params=_params("arbitrary"),
        name="mla_q",
    )(x, mods, mods, norm_g.reshape(1, d), cos, sin, w_dq, latent_g.reshape(1, Q_RANK), w_uq_heads)


def _attn_kernel(q_ref, k_ref, v_ref, mask_ref, o_ref, *, tq, nq, hg):
    qi = pl.program_id(2)
    nt = (((1,), (1,)), ((), ()))
    visible = mask_ref[...] != 0.0
    dv = v_ref.shape[-1]

    for i in range(nq):
        @pl.when(qi == i)
        def _(i=i):
            for hh in range(hg):
                q = q_ref[hh]
                m = l = acc = None
                for j in range(i, -1, -1):
                    lo = j * tq
                    s = lax.dot_general(q, k_ref[hh, lo:lo + tq, :], nt, preferred_element_type=F32)
                    if j == i:
                        s = jnp.where(visible, s, -1e30)
                    bm = jnp.max(s, axis=-1, keepdims=True)
                    m_new = bm if m is None else jnp.maximum(m, bm)
                    p = jnp.exp2(s - m_new)
                    ps = jnp.sum(p, axis=-1, keepdims=True)
                    pv = _dot(p.astype(BF16), v_ref[hh, lo:lo + tq, :])
                    if m is None:
                        l, acc = ps, pv
                    else:
                        alpha = jnp.exp2(m - m_new)
                        l = alpha * l + ps
                        acc = alpha * acc + pv
                    m = m_new
                o_ref[:, hh * dv:(hh + 1) * dv] = (acc / l).astype(BF16)


def _attention(q, k, v, tq=512, hg=2):
    nb, heads, seq, dq = q.shape
    dv = v.shape[-1]
    nq = seq // tq
    r = jnp.arange(tq, dtype=jnp.int32) // CHUNK
    mask = (r[None, :] <= r[:, None]).astype(F32)
    return pl.pallas_call(
        functools.partial(_attn_kernel, tq=tq, nq=nq, hg=hg),
        grid=(nb, heads // hg, nq),
        in_specs=[
            pl.BlockSpec((None, hg, tq, dq), lambda b, h, i: (b, h, i, 0)),
            pl.BlockSpec((None, hg, seq, dq), lambda b, h, i: (b, h, 0, 0)),
            pl.BlockSpec((None, hg, seq, dv), lambda b, h, i: (b, h, 0, 0)),
            pl.BlockSpec((tq, tq), lambda b, h, i: (0, 0)),
        ],
        out_specs=pl.BlockSpec((tq, hg * dv), lambda b, h, i: (b * nq + i, h)),
        out_shape=jax.ShapeDtypeStruct((nb * seq, heads * dv), BF16),
        compiler_params=_params("arbitrary", "arbitrary", "arbitrary"),
        name="mla_attn",
    )(q, k, v, mask)


def kernel(x, c, positions, ada_w, ada_b, norm_g, ffn_w_in, ffn_w_out, ret_w_in, ret_gn_g, ret_w_out, kv_ada_w, kv_ada_b, kv_norm_g, mla_w_dkv, kv_latent_g, mla_w_ukv, mla_w_dq, q_latent_g, mla_w_uq, mla_w_out, final_g):
    nb, seq, d = x.shape
    t = nb * seq
    depth = ada_w.shape[0]
    n_ret = ret_w_in.shape[0]
    ret_qk = d
    ret_dk = ret_qk // RET_HEADS
    ret_dv = ret_w_out.shape[1] // RET_HEADS

    xs = x.reshape(t, d)
    pos = positions.reshape(t, 1)
    c_pad = jnp.zeros((8, d), F32).at[:nb].set(c)

    half_ret = ret_dk // 2
    inv_ret = (ROPE_THETA ** (-jnp.arange(half_ret, dtype=F32) / half_ret)).reshape(1, half_ret)
    mla_cos, mla_sin = _mla_rope_tables(positions, MLA_ROPE // 2)
    log_g = jnp.log1p(-(2.0 ** (-5.0 - jnp.arange(RET_HEADS, dtype=F32))))
    log_g = jnp.broadcast_to(log_g[:, None, None], (RET_HEADS, 1, 128))

    ffn_in = ffn_w_in.astype(BF16)
    ffn_out = ffn_w_out.astype(BF16)
    ret_in_w = ret_w_in.astype(BF16)
    ret_out_w = ret_w_out.astype(BF16)
    mla_out_w = mla_w_out.astype(BF16)

    kq = kk = vv = None
    for l in range(depth):
        mods = _mod_table(_mods(c_pad, ada_w, ada_b, l), nb, d)
        if l == n_ret:
            kv_mods = _mod_table(_mods(c_pad, kv_ada_w[None], kv_ada_b[None], 0), nb, d)
            w_dkv = mla_w_dkv.astype(BF16)
            kk, vv = _shared_kv(xs, kv_mods, nb, seq, kv_norm_g, mla_cos, mla_sin,
                                w_dkv[:, :KV_RANK], w_dkv[:, KV_RANK:], kv_latent_g, mla_w_ukv.astype(BF16))
        xs = _ffn(xs, mods, 0, nb, norm_g[l, 0], ffn_in, ffn_out, l, 0, final_g, False)
        if l < n_ret:
            qkvg = _ret_in(xs, mods, 3, nb, norm_g[l, 1], pos, inv_ret, ret_in_w, l, ret_qk, ret_dk)
            y = _ret_core(qkvg, log_g, ret_gn_g[l], nb, seq, RET_HEADS, ret_dk, ret_dv)
            xs = _proj_res(y, ret_out_w, l, xs, mods, 5, nb)
        else:
            jj = l - n_ret
            dq = MLA_NOPE + MLA_ROPE
            w_uq = mla_w_uq[jj].astype(BF16).reshape(Q_RANK, MLA_HEADS, dq).transpose(1, 0, 2)
            kq = _mla_q(xs, mods, 3, nb, seq, norm_g[l, 1], mla_cos, mla_sin, mla_w_dq[jj].astype(BF16),
                        q_latent_g[jj], w_uq)
            y = _attention(kq, kk, vv)
            xs = _proj_res(y, mla_out_w, jj, xs, mods, 5, nb)
        xs = _ffn(xs, mods, 6, nb, norm_g[l, 2], ffn_in, ffn_out, l, 1, final_g, l == depth - 1)
    return xs.reshape(nb, seq, d)
```

```python
import functools
import math

import jax
import jax.numpy as jnp
from jax import lax
from jax.experimental import pallas as pl
from jax.experimental.pallas import tpu as pltpu

CHUNK = 64
RET_HEADS = 8
MLA_HEADS = 16
MLA_NOPE = 128
MLA_ROPE = 64
MLA_V = 128
Q_RANK = 512
KV_RANK = 512
ROPE_THETA = 10000.0
EPS = 1e-6
N_MOD = 9

BF16 = jnp.bfloat16
F32 = jnp.float32

VMEM_LIMIT_BYTES = 56 * 1024 * 1024


def _params(*sem):
    return pltpu.CompilerParams(dimension_semantics=sem, vmem_limit_bytes=VMEM_LIMIT_BYTES)


def _silu(x):
    return x / (1.0 + jnp.exp(-x))


def _dot(a, b):
    return jnp.dot(a, b, preferred_element_type=F32)


def _modulated_norm(x, g, shift, scale):
    ms = jnp.mean(x * x, axis=-1, keepdims=True)
    return (x * lax.rsqrt(ms + EPS)) * g * (1.0 + scale) + shift


def _rope_tables(pos_ref, inv_ref):
    ang = pos_ref[...].astype(F32) * inv_ref[...]
    return jnp.cos(ang), jnp.sin(ang)


def _mod_spec(k, nb, blocks_per_batch, d):
    return pl.BlockSpec((None, 1, d), lambda i, *_: (k * nb + i // blocks_per_batch, 0, 0))


def _mods_kernel(c_ref, w_ref, b_ref, o_ref):
    ca = _silu(c_ref[...]).astype(BF16)
    o_ref[...] = _dot(ca, w_ref[...].astype(BF16)) + b_ref[...]


def _mods(c_pad, w, b, layer, tn=1024):
    rows, d = c_pad.shape
    nl, _, n = w.shape
    return pl.pallas_call(
        _mods_kernel,
        grid=(n // tn,),
        in_specs=[
            pl.BlockSpec((rows, d), lambda j: (0, 0)),
            pl.BlockSpec((None, d, tn), lambda j: (layer, 0, j)),
            pl.BlockSpec((None, 1, tn), lambda j: (layer, 0, j)),
        ],
        out_specs=pl.BlockSpec((rows, tn), lambda j: (0, j)),
        out_shape=jax.ShapeDtypeStruct((rows, n), F32),
        compiler_params=_params("arbitrary"),
        name="ada_mods",
    )(c_pad, w, b.reshape(nl, 1, n))


def _mod_table(m, nb, d):
    k = m.shape[1] // d
    return m[:nb].reshape(nb, k, d).transpose(1, 0, 2).reshape(k * nb, 1, d)


def _ffn_kernel(x_ref, sh_ref, sc_ref, gt_ref, g_ref, wg_ref, wu_ref, wo_ref, fg_ref, *rest,
                final_norm, row_splits, cast_next):
    if cast_next:
        nin_ref, nout_ref, o_ref, nin_bf_ref, nout_bf_ref, h_ref = rest
        nin_bf_ref[...] = nin_ref[...].astype(BF16)
        nout_bf_ref[...] = nout_ref[...].astype(BF16)
    else:
        o_ref, h_ref = rest
    j = pl.program_id(1)
    rs = x_ref.shape[0] // row_splits
    slabs = [slice(r * rs, (r + 1) * rs) for r in range(row_splits)]

    def per_slab(fn):
        def body(r, carry):
            fn(pl.ds(pl.multiple_of(r * rs, rs), rs))
            return carry
        lax.fori_loop(0, row_splits, body, 0)

    @pl.when(j == 0)
    def _():
        def prologue(rows):
            h = _modulated_norm(x_ref[rows, :], g_ref[...], sh_ref[...], sc_ref[...])
            h_ref[rows, :] = h.astype(BF16)
            o_ref[rows, :] = jnp.zeros((rs, o_ref.shape[1]), F32)
        per_slab(prologue)

    for rows in slabs:
        h = h_ref[rows, :]
        gate = _dot(h, wg_ref[...])
        up = _dot(h, wu_ref[...])
        a = (_silu(gate) * up).astype(BF16)
        o_ref[rows, :] += _dot(a, wo_ref[...])

    @pl.when(j == pl.num_programs(1) - 1)
    def _():
        def epilogue(rows):
            y = x_ref[rows, :] + (0.5 * gt_ref[...]) * o_ref[rows, :]
            if final_norm:
                ms = jnp.mean(y * y, axis=-1, keepdims=True)
                y = (y * lax.rsqrt(ms + EPS)) * fg_ref[...]
            o_ref[rows, :] = y
        per_slab(epilogue)


def _ffn(x, mods, mod_base, nb, norm_g, w_in, w_out, final_g, final_norm, next_w=None, tm=1024, tf=512):
    t, d = x.shape
    dff = w_out.shape[0]
    nf = dff // tf
    ni = t // tm
    bpb = (t // nb) // tm
    in_specs = [
        pl.BlockSpec((tm, d), lambda i, j: (i, 0)),
        _mod_spec(mod_base + 0, nb, bpb, d),
        _mod_spec(mod_base + 1, nb, bpb, d),
        _mod_spec(mod_base + 2, nb, bpb, d),
        pl.BlockSpec((1, d), lambda i, j: (0, 0)),
        pl.BlockSpec((d, tf), lambda i, j: (0, j)),
        pl.BlockSpec((d, tf), lambda i, j: (0, nf + j)),
        pl.BlockSpec((tf, d), lambda i, j: (j, 0)),
        pl.BlockSpec((1, d), lambda i, j: (0, 0)),
    ]
    args = [x, mods, mods, mods, norm_g.reshape(1, d), w_in, w_in, w_out, final_g.reshape(1, d)]
    out_specs = [pl.BlockSpec((tm, d), lambda i, j: (i, 0))]
    out_shape = [jax.ShapeDtypeStruct((t, d), F32)]
    if next_w is not None:
        nin, nout, nl, nk = next_w
        in_blk = (d // ni, 2 * dff // nf)
        out_blk = (dff // nf, d // ni)
        in_specs += [pl.BlockSpec((None, None) + in_blk, lambda i, j: (nl, nk, i, j)),
                     pl.BlockSpec((None, None) + out_blk, lambda i, j: (nl, nk, j, i))]
        args += [nin, nout]
        out_specs += [pl.BlockSpec(in_blk, lambda i, j: (i, j)), pl.BlockSpec(out_blk, lambda i, j: (j, i))]
        out_shape += [jax.ShapeDtypeStruct((d, 2 * dff), BF16), jax.ShapeDtypeStruct((dff, d), BF16)]
    res = pl.pallas_call(
        functools.partial(_ffn_kernel, final_norm=final_norm, row_splits=4, cast_next=next_w is not None),
        grid=(ni, nf),
        in_specs=in_specs,
        out_specs=out_specs,
        out_shape=out_shape,
        scratch_shapes=[pltpu.VMEM((tm, d), BF16)],
        compiler_params=_params("arbitrary", "arbitrary"),
        name="ffn",
    )(*args)
    return res[0], (tuple(res[1:]) if next_w is not None else None)


def _ret_in_kernel(x_ref, sh_ref, sc_ref, g_ref, pos_ref, inv_ref, w_ref, o_ref, h_ref, cos_ref, sin_ref,
                   *, n_q_tiles, dk, q_scale):
    j = pl.program_id(1)

    @pl.when(j == 0)
    def _():
        h = _modulated_norm(x_ref[...], g_ref[...], sh_ref[...], sc_ref[...])
        h_ref[...] = h.astype(BF16)
        cos, sin = _rope_tables(pos_ref, inv_ref)
        cos_ref[...] = cos
        sin_ref[...] = sin

    y = _dot(h_ref[...], w_ref[...])
    tn = y.shape[1]
    half = dk // 2

    def roped(scale):
        cos = cos_ref[...]
        sin = sin_ref[...]
        for hd in range(tn // dk):
            x1 = y[:, hd * dk:hd * dk + half]
            x2 = y[:, hd * dk + half:(hd + 1) * dk]
            o_ref[:, hd * dk:hd * dk + half] = ((x1 * cos - x2 * sin) * scale).astype(BF16)
            o_ref[:, hd * dk + half:(hd + 1) * dk] = ((x1 * sin + x2 * cos) * scale).astype(BF16)

    @pl.when(j < n_q_tiles)
    def _():
        roped(q_scale)

    @pl.when(jnp.logical_and(j >= n_q_tiles, j < 2 * n_q_tiles))
    def _():
        roped(1.0)

    @pl.when(j >= 2 * n_q_tiles)
    def _():
        o_ref[...] = y.astype(BF16)


def _ret_in(x, mods, mod_base, nb, norm_g, pos, inv, w, layer, qk_dim, dk, tm=1024, tn=1024):
    t, d = x.shape
    n = w.shape[2]
    bpb = (t // nb) // tm
    half = dk // 2
    return pl.pallas_call(
        functools.partial(_ret_in_kernel, n_q_tiles=qk_dim // tn, dk=dk, q_scale=dk ** -0.5),
        grid=(t // tm, n // tn),
        in_specs=[
            pl.BlockSpec((tm, d), lambda i, j: (i, 0)),
            _mod_spec(mod_base + 0, nb, bpb, d),
            _mod_spec(mod_base + 1, nb, bpb, d),
            pl.BlockSpec((1, d), lambda i, j: (0, 0)),
            pl.BlockSpec((tm, 1), lambda i, j: (i, 0)),
            pl.BlockSpec((1, half), lambda i, j: (0, 0)),
            pl.BlockSpec((None, d, tn), lambda i, j: (layer, 0, j)),
        ],
        out_specs=pl.BlockSpec((tm, tn), lambda i, j: (i, j)),
        out_shape=jax.ShapeDtypeStruct((t, n), BF16),
        scratch_shapes=[pltpu.VMEM((tm, d), BF16), pltpu.VMEM((tm, half), F32), pltpu.VMEM((tm, half), F32)],
        compiler_params=_params("arbitrary", "arbitrary"),
        name="ret_in",
    )(x, mods, mods, norm_g.reshape(1, d), pos, inv, w)


def _ret_core_kernel(lg_ref, q_ref, k_ref, v_ref, g_ref, gn_ref, o_ref, state_ref, dmask_ref, *, span, hg, dk, dv):
    i = pl.program_id(2)

    @pl.when(i == 0)
    def _():
        state_ref[...] = jnp.zeros_like(state_ref)
        r = lax.broadcasted_iota(jnp.int32, (span, span), 0)
        c = lax.broadcasted_iota(jnp.int32, (span, span), 1)
        dist = jnp.abs(r - c).astype(F32)
        causal = (c // CHUNK) <= (r // CHUNK)
        for hh in range(hg):
            dmask_ref[hh] = jnp.where(causal, jnp.exp(lg_ref[hh][:, :1] * dist), 0.0)

    idx = lax.broadcasted_iota(jnp.int32, (span, 1), 0).astype(F32)
    for hh in range(hg):
        lg = lg_ref[hh][:, :1]
        xi = jnp.exp(lg * (idx + 1.0))
        zeta = jnp.exp(lg * (span - 1.0 - idx))
        g_span = jnp.exp(lg * float(span))

        q = q_ref[:, hh * dk:(hh + 1) * dk]
        k = k_ref[:, hh * dk:(hh + 1) * dk]
        v = v_ref[:, hh * dv:(hh + 1) * dv]
        s = lax.dot_general(q, k, (((1,), (1,)), ((), ())), preferred_element_type=F32) * dmask_ref[hh]
        state = state_ref[hh]
        o = _dot(s.astype(BF16), v) + _dot((q.astype(F32) * xi).astype(BF16), state.astype(BF16))
        kz = (k.astype(F32) * zeta).astype(BF16)
        state_ref[hh] = state * g_span + lax.dot_general(kz, v, (((0,), (0,)), ((), ())),
                                                         preferred_element_type=F32)

        mu = jnp.mean(o, axis=-1, keepdims=True)
        oc = o - mu
        var = jnp.mean(oc * oc, axis=-1, keepdims=True)
        on = (oc * lax.rsqrt(var + EPS)) * gn_ref[:, hh * dv:(hh + 1) * dv]
        gate = _silu(g_ref[:, hh * dv:(hh + 1) * dv].astype(F32))
        o_ref[:, hh * dv:(hh + 1) * dv] = (gate * on).astype(BF16)


def _ret_core(qkvg, log_g, gn_g, nb, seq, heads, dk, dv, span=512, hg=2):
    t = qkvg.shape[0]
    ns = seq // span
    groups = heads // hg
    k_off = groups
    v_off = (2 * heads * dk) // (hg * dv)
    g_off = v_off + groups
    row = lambda b, h, i: b * ns + i
    return pl.pallas_call(
        functools.partial(_ret_core_kernel, span=span, hg=hg, dk=dk, dv=dv),
        grid=(nb, groups, ns),
        in_specs=[
            pl.BlockSpec((hg, 1, 128), lambda b, h, i: (h, 0, 0)),
            pl.BlockSpec((span, hg * dk), lambda b, h, i: (row(b, h, i), h)),
            pl.BlockSpec((span, hg * dk), lambda b, h, i: (row(b, h, i), k_off + h)),
            pl.BlockSpec((span, hg * dv), lambda b, h, i: (row(b, h, i), v_off + h)),
            pl.BlockSpec((span, hg * dv), lambda b, h, i: (row(b, h, i), g_off + h)),
            pl.BlockSpec((1, hg * dv), lambda b, h, i: (0, h)),
        ],
        out_specs=pl.BlockSpec((span, hg * dv), lambda b, h, i: (row(b, h, i), h)),
        out_shape=jax.ShapeDtypeStruct((t, heads * dv), BF16),
        scratch_shapes=[pltpu.VMEM((hg, dk, dv), F32), pltpu.VMEM((hg, span, span), F32)],
        compiler_params=_params("arbitrary", "arbitrary", "arbitrary"),
        name="ret_core",
    )(log_g, qkvg, qkvg, qkvg, qkvg, gn_g.reshape(1, heads * dv))


def _proj_res_kernel(y_ref, w_ref, x_ref, gt_ref, o_ref):
    o_ref[...] = x_ref[...] + gt_ref[...] * _dot(y_ref[...], w_ref[...])


def _proj_res(y, w, layer, x, mods, mod_idx, nb, tm=1024, tn=512):
    t, kdim = y.shape
    d = w.shape[2]
    bpb = (t // nb) // tm
    return pl.pallas_call(
        _proj_res_kernel,
        grid=(t // tm, d // tn),
        in_specs=[
            pl.BlockSpec((tm, kdim), lambda i, j: (i, 0)),
            pl.BlockSpec((None, kdim, tn), lambda i, j: (layer, 0, j)),
            pl.BlockSpec((tm, tn), lambda i, j: (i, j)),
            pl.BlockSpec((None, 1, tn), lambda i, j: (mod_idx * nb + i // bpb, 0, j)),
        ],
        out_specs=pl.BlockSpec((tm, tn), lambda i, j: (i, j)),
        out_shape=jax.ShapeDtypeStruct((t, d), F32),
        compiler_params=_params("arbitrary", "arbitrary"),
        name="proj_res",
    )(y, w, x, mods)


def _rope_table_kernel(pos_ref, inv_ref, cos_ref, sin_ref):
    cos, sin = _rope_tables(pos_ref, inv_ref)
    cos_ref[...] = cos
    sin_ref[...] = sin


def _mla_rope_tables(positions, half, tr=512):
    t = positions.size
    per_row = 128 // half
    rows = t // per_row
    pos_rep = jnp.repeat(positions.reshape(rows, per_row), half, axis=1)
    inv = ROPE_THETA ** (-jnp.arange(half, dtype=F32) / half)
    inv_rep = jnp.tile(inv, per_row).reshape(1, 128)
    cos, sin = pl.pallas_call(
        _rope_table_kernel,
        grid=(rows // tr,),
        in_specs=[pl.BlockSpec((tr, 128), lambda i: (i, 0)), pl.BlockSpec((1, 128), lambda i: (0, 0))],
        out_specs=[pl.BlockSpec((tr, 128), lambda i: (i, 0))] * 2,
        out_shape=[jax.ShapeDtypeStruct((rows, 128), F32)] * 2,
        compiler_params=_params("arbitrary"),
        name="mla_rope_tables",
    )(pos_rep, inv_rep)
    return cos.reshape(t, half), sin.reshape(t, half)


def _rope_tail(x, cos, sin):
    half = MLA_ROPE // 2
    x1 = x[:, :half]
    x2 = x[:, half:]
    return jnp.concatenate([x1 * cos - x2 * sin, x1 * sin + x2 * cos], axis=-1)


def _kv_kernel(x_ref, sh_ref, sc_ref, g_ref, cos_ref, sin_ref, wdc_ref, wdr_ref, lg_ref, wu_ref, k_ref, v_ref):
    h = _modulated_norm(x_ref[...], g_ref[...], sh_ref[...], sc_ref[...]).astype(BF16)
    ckv = _dot(h, wdc_ref[...])
    ms = jnp.mean(ckv * ckv, axis=-1, keepdims=True)
    ckv = ((ckv * lax.rsqrt(ms + EPS)) * lg_ref[...]).astype(BF16)
    kr = _rope_tail(_dot(h, wdr_ref[...]), cos_ref[...], sin_ref[...]).astype(BF16)
    width = MLA_NOPE + MLA_V
    for hd in range(MLA_HEADS):
        kv = _dot(ckv, wu_ref[:, hd * width:(hd + 1) * width])
        k_ref[hd, :, :MLA_NOPE] = kv[:, :MLA_NOPE].astype(BF16)
        k_ref[hd, :, MLA_NOPE:] = kr
        v_ref[hd] = kv[:, MLA_NOPE:].astype(BF16)


def _shared_kv(x, mods, nb, seq, norm_g, cos, sin, w_dc, w_dr, latent_g, w_ukv, tm=512):
    t, d = x.shape
    spb = seq // tm
    const = lambda i: (0, 0)
    return pl.pallas_call(
        _kv_kernel,
        grid=(t // tm,),
        in_specs=[
            pl.BlockSpec((tm, d), lambda i: (i, 0)),
            _mod_spec(0, nb, spb, d),
            _mod_spec(1, nb, spb, d),
            pl.BlockSpec((1, d), const),
            pl.BlockSpec((tm, MLA_ROPE // 2), lambda i: (i, 0)),
            pl.BlockSpec((tm, MLA_ROPE // 2), lambda i: (i, 0)),
            pl.BlockSpec(w_dc.shape, const),
            pl.BlockSpec(w_dr.shape, const),
            pl.BlockSpec((1, KV_RANK), const),
            pl.BlockSpec(w_ukv.shape, const),
        ],
        out_specs=[
            pl.BlockSpec((None, MLA_HEADS, tm, MLA_NOPE + MLA_ROPE), lambda i: (i // spb, 0, i % spb, 0)),
            pl.BlockSpec((None, MLA_HEADS, tm, MLA_V), lambda i: (i // spb, 0, i % spb, 0)),
        ],
        out_shape=[
            jax.ShapeDtypeStruct((nb, MLA_HEADS, seq, MLA_NOPE + MLA_ROPE), BF16),
            jax.ShapeDtypeStruct((nb, MLA_HEADS, seq, MLA_V), BF16),
        ],
        compiler_params=_params("arbitrary"),
        name="mla_kv",
    )(x, mods, mods, norm_g.reshape(1, d), cos, sin, w_dc, w_dr, latent_g.reshape(1, KV_RANK), w_ukv)


def _q_kernel(x_ref, sh_ref, sc_ref, g_ref, cos_ref, sin_ref, wd_ref, lg_ref, wu_ref, q_ref, *, q_scale):
    h = _modulated_norm(x_ref[...], g_ref[...], sh_ref[...], sc_ref[...]).astype(BF16)
    cq = _dot(h, wd_ref[...])
    ms = jnp.mean(cq * cq, axis=-1, keepdims=True)
    cq = ((cq * lax.rsqrt(ms + EPS)) * lg_ref[...]).astype(BF16)
    cos = cos_ref[...]
    sin = sin_ref[...]
    for hd in range(MLA_HEADS):
        qh = _dot(cq, wu_ref[hd]) * q_scale
        q_ref[hd, :, :MLA_NOPE] = qh[:, :MLA_NOPE].astype(BF16)
        q_ref[hd, :, MLA_NOPE:] = _rope_tail(qh[:, MLA_NOPE:], cos, sin).astype(BF16)


def _mla_q(x, mods, mod_base, nb, seq, norm_g, cos, sin, w_dq, latent_g, w_uq_heads, tm=512):
    t, d = x.shape
    spb = seq // tm
    const = lambda i: (0, 0)
    dq = MLA_NOPE + MLA_ROPE
    return pl.pallas_call(
        functools.partial(_q_kernel, q_scale=dq ** -0.5 * math.log2(math.e)),
        grid=(t // tm,),
        in_specs=[
            pl.BlockSpec((tm, d), lambda i: (i, 0)),
            _mod_spec(mod_base + 0, nb, spb, d),
            _mod_spec(mod_base + 1, nb, spb, d),
            pl.BlockSpec((1, d), const),
            pl.BlockSpec((tm, MLA_ROPE // 2), lambda i: (i, 0)),
            pl.BlockSpec((tm, MLA_ROPE // 2), lambda i: (i, 0)),
            pl.BlockSpec(w_dq.shape, const),
            pl.BlockSpec((1, Q_RANK), const),
            pl.BlockSpec(w_uq_heads.shape, lambda i: (0, 0, 0)),
        ],
        out_specs=pl.BlockSpec((None, MLA_HEADS, tm, dq), lambda i: (i // spb, 0, i % spb, 0)),
        out_shape=jax.ShapeDtypeStruct((nb, MLA_HEADS, seq, dq), BF16),
        compiler_params=_params("arbitrary"),
        name="mla_q",
    )(x, mods, mods, norm_g.reshape(1, d), cos, sin, w_dq, latent_g.reshape(1, Q_RANK), w_uq_heads)


def _attn_kernel(q_ref, k_ref, v_ref, mask_ref, o_ref, *, tq, nq, hg):
    qi = pl.program_id(2)
    nt = (((1,), (1,)), ((), ()))
    visible = mask_ref[...] != 0.0
    dv = v_ref.shape[-1]

    for i in range(nq):
        @pl.when(qi == i)
        def _(i=i):
            for hh in range(hg):
                q = q_ref[hh]
                m = l = acc = None
                for j in range(i, -1, -1):
                    lo = j * tq
                    s = lax.dot_general(q, k_ref[hh, lo:lo + tq, :], nt, preferred_element_type=F32)
                    if j == i:
                        s = jnp.where(visible, s, -1e30)
                    bm = jnp.max(s, axis=-1, keepdims=True)
                    m_new = bm if m is None else jnp.maximum(m, bm)
                    p = jnp.exp2(s - m_new)
                    ps = jnp.sum(p, axis=-1, keepdims=True)
                    pv = _dot(p.astype(BF16), v_ref[hh, lo:lo + tq, :])
                    if m is None:
                        l, acc = ps, pv
                    else:
                        alpha = jnp.exp2(m - m_new)
                        l = alpha * l + ps
                        acc = alpha * acc + pv
                    m = m_new
                o_ref[:, hh * dv:(hh + 1) * dv] = (acc / l).astype(BF16)


def _attention(q, k, v, tq=512, hg=2):
    nb, heads, seq, dq = q.shape
    dv = v.shape[-1]
    nq = seq // tq
    r = jnp.arange(tq, dtype=jnp.int32) // CHUNK
    mask = (r[None, :] <= r[:, None]).astype(F32)
    return pl.pallas_call(
        functools.partial(_attn_kernel, tq=tq, nq=nq, hg=hg),
        grid=(nb, heads // hg, nq),
        in_specs=[
            pl.BlockSpec((None, hg, tq, dq), lambda b, h, i: (b, h, i, 0)),
            pl.BlockSpec((None, hg, seq, dq), lambda b, h, i: (b, h, 0, 0)),
            pl.BlockSpec((None, hg, seq, dv), lambda b, h, i: (b, h, 0, 0)),
            pl.BlockSpec((tq, tq), lambda b, h, i: (0, 0)),
        ],
        out_specs=pl.BlockSpec((tq, hg * dv), lambda b, h, i: (b * nq + i, h)),
        out_shape=jax.ShapeDtypeStruct((nb * seq, heads * dv), BF16),
        compiler_params=_params("arbitrary", "arbitrary", "arbitrary"),
        name="mla_attn",
    )(q, k, v, mask)


def kernel(x, c, positions, ada_w, ada_b, norm_g, ffn_w_in, ffn_w_out, ret_w_in, ret_gn_g, ret_w_out, kv_ada_w, kv_ada_b, kv_norm_g, mla_w_dkv, kv_latent_g, mla_w_ukv, mla_w_dq, q_latent_g, mla_w_uq, mla_w_out, final_g):
    nb, seq, d = x.shape
    t = nb * seq
    depth = ada_w.shape[0]
    n_ret = ret_w_in.shape[0]
    ret_qk = d
    ret_dk = ret_qk // RET_HEADS
    ret_dv = ret_w_out.shape[1] // RET_HEADS

    xs = x.reshape(t, d)
    pos = positions.reshape(t, 1)
    c_pad = jnp.zeros((8, d), F32).at[:nb].set(c)

    half_ret = ret_dk // 2
    inv_ret = (ROPE_THETA ** (-jnp.arange(half_ret, dtype=F32) / half_ret)).reshape(1, half_ret)
    mla_cos, mla_sin = _mla_rope_tables(positions, MLA_ROPE // 2)
    log_g = jnp.log1p(-(2.0 ** (-5.0 - jnp.arange(RET_HEADS, dtype=F32))))
    log_g = jnp.broadcast_to(log_g[:, None, None], (RET_HEADS, 1, 128))

    ffn_order = [(l, k) for l in range(depth) for k in range(2)]
    ffn_w = (ffn_w_in[0, 0].astype(BF16), ffn_w_out[0, 0].astype(BF16))

    def ffn(xs, mods, mod_base, l, k, ffn_w):
        n = ffn_order.index((l, k))
        nxt = (ffn_w_in, ffn_w_out) + ffn_order[n + 1] if n + 1 < len(ffn_order) else None
        return _ffn(xs, mods, mod_base, nb, norm_g[l, 2 * k], ffn_w[0], ffn_w[1], final_g,
                    n + 1 == len(ffn_order), nxt)

    ret_in_w = ret_w_in.astype(BF16)
    ret_out_w = ret_w_out.astype(BF16)
    mla_out_w = mla_w_out.astype(BF16)

    kq = kk = vv = None
    for l in range(depth):
        mods = _mod_table(_mods(c_pad, ada_w, ada_b, l), nb, d)
        if l == n_ret:
            kv_mods = _mod_table(_mods(c_pad, kv_ada_w[None], kv_ada_b[None], 0), nb, d)
            w_dkv = mla_w_dkv.astype(BF16)
            kk, vv = _shared_kv(xs, kv_mods, nb, seq, kv_norm_g, mla_cos, mla_sin,
                                w_dkv[:, :KV_RANK], w_dkv[:, KV_RANK:], kv_latent_g, mla_w_ukv.astype(BF16))
        xs, ffn_w = ffn(xs, mods, 0, l, 0, ffn_w)
        if l < n_ret:
            qkvg = _ret_in(xs, mods, 3, nb, norm_g[l, 1], pos, inv_ret, ret_in_w, l, ret_qk, ret_dk)
            y = _ret_core(qkvg, log_g, ret_gn_g[l], nb, seq, RET_HEADS, ret_dk, ret_dv)
            xs = _proj_res(y, ret_out_w, l, xs, mods, 5, nb)
        else:
            jj = l - n_ret
            dq = MLA_NOPE + MLA_ROPE
            w_uq = mla_w_uq[jj].astype(BF16).reshape(Q_RANK, MLA_HEADS, dq).transpose(1, 0, 2)
            kq = _mla_q(xs, mods, 3, nb, seq, norm_g[l, 1], mla_cos, mla_sin, mla_w_dq[jj].astype(BF16),
                        q_latent_g[jj], w_uq)
            y = _attention(kq, kk, vv)
            xs = _proj_res(y, mla_out_w, jj, xs, mods, 5, nb)
        xs, ffn_w = ffn(xs, mods, 6, l, 1, ffn_w)
    return xs.reshape(nb, seq, d)
```

```python
import functools
import math

import jax
import jax.numpy as jnp
from jax import lax
from jax.experimental import pallas as pl
from jax.experimental.pallas import tpu as pltpu

CHUNK = 64
RET_HEADS = 8
MLA_HEADS = 16
MLA_NOPE = 128
MLA_ROPE = 64
MLA_V = 128
Q_RANK = 512
KV_RANK = 512
ROPE_THETA = 10000.0
EPS = 1e-6
N_MOD = 9

BF16 = jnp.bfloat16
F32 = jnp.float32

VMEM_LIMIT_BYTES = 56 * 1024 * 1024


def _params(*sem):
    return pltpu.CompilerParams(dimension_semantics=sem, vmem_limit_bytes=VMEM_LIMIT_BYTES)


def _silu(x):
    return x / (1.0 + jnp.exp(-x))


def _dot(a, b):
    return jnp.dot(a, b, preferred_element_type=F32)


def _modulated_norm(x, g, shift, scale):
    ms = jnp.mean(x * x, axis=-1, keepdims=True)
    return (x * lax.rsqrt(ms + EPS)) * g * (1.0 + scale) + shift


def _rope_tables(pos_ref, inv_ref):
    ang = pos_ref[...].astype(F32) * inv_ref[...]
    return jnp.cos(ang), jnp.sin(ang)


def _mod_spec(k, nb, blocks_per_batch, d):
    return pl.BlockSpec((None, 1, d), lambda i, *_: (k * nb + i // blocks_per_batch, 0, 0))


def _mods_kernel(c_ref, w_ref, b_ref, o_ref):
    ca = _silu(c_ref[...]).astype(BF16)
    o_ref[...] = _dot(ca, w_ref[...].astype(BF16)) + b_ref[...]


def _mods(c_pad, w, b, layer, tn=1024):
    rows, d = c_pad.shape
    nl, _, n = w.shape
    return pl.pallas_call(
        _mods_kernel,
        grid=(n // tn,),
        in_specs=[
            pl.BlockSpec((rows, d), lambda j: (0, 0)),
            pl.BlockSpec((None, d, tn), lambda j: (layer, 0, j)),
            pl.BlockSpec((None, 1, tn), lambda j: (layer, 0, j)),
        ],
        out_specs=pl.BlockSpec((rows, tn), lambda j: (0, j)),
        out_shape=jax.ShapeDtypeStruct((rows, n), F32),
        compiler_params=_params("arbitrary"),
        name="ada_mods",
    )(c_pad, w, b.reshape(nl, 1, n))


def _mod_table(m, nb, d):
    k = m.shape[1] // d
    return m[:nb].reshape(nb, k, d).transpose(1, 0, 2).reshape(k * nb, 1, d)


def _ffn_kernel(x_ref, sh_ref, sc_ref, gt_ref, g_ref, wg_ref, wu_ref, wo_ref, fg_ref, *rest,
                final_norm, row_splits, cast_next):
    if cast_next:
        nin_ref, nout_ref, o_ref, nin_bf_ref, nout_bf_ref, h_ref, inv_ref = rest
        nin_bf_ref[...] = nin_ref[...].astype(BF16)
        nout_bf_ref[...] = nout_ref[...].astype(BF16)
    else:
        o_ref, h_ref, inv_ref = rest
    j = pl.program_id(1)
    rs = x_ref.shape[0] // row_splits
    slabs = [slice(r * rs, (r + 1) * rs) for r in range(row_splits)]

    def per_slab(fn):
        def body(r, carry):
            fn(pl.ds(pl.multiple_of(r * rs, rs), rs))
            return carry
        lax.fori_loop(0, row_splits, body, 0)

    def step(first):
        for rows in slabs:
            h = h_ref[rows, :]
            gate = _dot(h, wg_ref[...])
            up = _dot(h, wu_ref[...])
            a = (_silu(gate) * up).astype(BF16)
            y = _dot(a, wo_ref[...])
            if first:
                o_ref[rows, :] = y
            else:
                o_ref[rows, :] += y

    @pl.when(j == 0)
    def _():
        def moments(rows):
            x = x_ref[rows, :]
            inv_ref[rows, :] = lax.rsqrt(jnp.mean(x * x, axis=-1, keepdims=True) + EPS)

        def scale(rows):
            h = (x_ref[rows, :] * inv_ref[rows, :]) * g_ref[...] * (1.0 + sc_ref[...]) + sh_ref[...]
            h_ref[rows, :] = h.astype(BF16)
        per_slab(moments)
        per_slab(scale)
        step(True)

    @pl.when(j > 0)
    def _():
        step(False)

    @pl.when(j == pl.num_programs(1) - 1)
    def _():
        def residual(rows):
            y = x_ref[rows, :] + (0.5 * gt_ref[...]) * o_ref[rows, :]
            o_ref[rows, :] = y
            if final_norm:
                inv_ref[rows, :] = lax.rsqrt(jnp.mean(y * y, axis=-1, keepdims=True) + EPS)

        def final_scale(rows):
            o_ref[rows, :] = (o_ref[rows, :] * inv_ref[rows, :]) * fg_ref[...]
        per_slab(residual)
        if final_norm:
            per_slab(final_scale)


def _ffn(x, mods, mod_base, nb, norm_g, w_in, w_out, final_g, final_norm, next_w=None, tm=1024, tf=512):
    t, d = x.shape
    dff = w_out.shape[0]
    nf = dff // tf
    ni = t // tm
    bpb = (t // nb) // tm
    in_specs = [
        pl.BlockSpec((tm, d), lambda i, j: (i, 0)),
        _mod_spec(mod_base + 0, nb, bpb, d),
        _mod_spec(mod_base + 1, nb, bpb, d),
        _mod_spec(mod_base + 2, nb, bpb, d),
        pl.BlockSpec((1, d), lambda i, j: (0, 0)),
        pl.BlockSpec((d, tf), lambda i, j: (0, j)),
        pl.BlockSpec((d, tf), lambda i, j: (0, nf + j)),
        pl.BlockSpec((tf, d), lambda i, j: (j, 0)),
        pl.BlockSpec((1, d), lambda i, j: (0, 0)),
    ]
    args = [x, mods, mods, mods, norm_g.reshape(1, d), w_in, w_in, w_out, final_g.reshape(1, d)]
    out_specs = [pl.BlockSpec((tm, d), lambda i, j: (i, 0))]
    out_shape = [jax.ShapeDtypeStruct((t, d), F32)]
    if next_w is not None:
        nin, nout, nl, nk = next_w
        in_blk = (d // ni, 2 * dff // nf)
        out_blk = (dff // nf, d // ni)
        in_specs += [pl.BlockSpec((None, None) + in_blk, lambda i, j: (nl, nk, i, j)),
                     pl.BlockSpec((None, None) + out_blk, lambda i, j: (nl, nk, j, i))]
        args += [nin, nout]
        out_specs += [pl.BlockSpec(in_blk, lambda i, j: (i, j)), pl.BlockSpec(out_blk, lambda i, j: (j, i))]
        out_shape += [jax.ShapeDtypeStruct((d, 2 * dff), BF16), jax.ShapeDtypeStruct((dff, d), BF16)]
    res = pl.pallas_call(
        functools.partial(_ffn_kernel, final_norm=final_norm, row_splits=4, cast_next=next_w is not None),
        grid=(ni, nf),
        in_specs=in_specs,
        out_specs=out_specs,
        out_shape=out_shape,
        scratch_shapes=[pltpu.VMEM((tm, d), BF16), pltpu.VMEM((tm, 1), F32)],
        compiler_params=_params("arbitrary", "arbitrary"),
        name="ffn",
    )(*args)
    return res[0], (tuple(res[1:]) if next_w is not None else None)


def _ret_in_kernel(x_ref, sh_ref, sc_ref, g_ref, pos_ref, inv_ref, w_ref, o_ref, h_ref, cos_ref, sin_ref,
                   *, n_q_tiles, dk, q_scale):
    j = pl.program_id(1)

    @pl.when(j == 0)
    def _():
        h = _modulated_norm(x_ref[...], g_ref[...], sh_ref[...], sc_ref[...])
        h_ref[...] = h.astype(BF16)
        cos, sin = _rope_tables(pos_ref, inv_ref)
        cos_ref[...] = cos
        sin_ref[...] = sin

    y = _dot(h_ref[...], w_ref[...])
    tn = y.shape[1]
    half = dk // 2

    def roped(scale):
        cos = cos_ref[...]
        sin = sin_ref[...]
        for hd in range(tn // dk):
            x1 = y[:, hd * dk:hd * dk + half]
            x2 = y[:, hd * dk + half:(hd + 1) * dk]
            o_ref[:, hd * dk:hd * dk + half] = ((x1 * cos - x2 * sin) * scale).astype(BF16)
            o_ref[:, hd * dk + half:(hd + 1) * dk] = ((x1 * sin + x2 * cos) * scale).astype(BF16)

    @pl.when(j < n_q_tiles)
    def _():
        roped(q_scale)

    @pl.when(jnp.logical_and(j >= n_q_tiles, j < 2 * n_q_tiles))
    def _():
        roped(1.0)

    @pl.when(j >= 2 * n_q_tiles)
    def _():
        o_ref[...] = y.astype(BF16)


def _ret_in(x, mods, mod_base, nb, norm_g, pos, inv, w, layer, qk_dim, dk, tm=1024, tn=1024):
    t, d = x.shape
    n = w.shape[2]
    bpb = (t // nb) // tm
    half = dk // 2
    return pl.pallas_call(
        functools.partial(_ret_in_kernel, n_q_tiles=qk_dim // tn, dk=dk, q_scale=dk ** -0.5),
        grid=(t // tm, n // tn),
        in_specs=[
            pl.BlockSpec((tm, d), lambda i, j: (i, 0)),
            _mod_spec(mod_base + 0, nb, bpb, d),
            _mod_spec(mod_base + 1, nb, bpb, d),
            pl.BlockSpec((1, d), lambda i, j: (0, 0)),
            pl.BlockSpec((tm, 1), lambda i, j: (i, 0)),
            pl.BlockSpec((1, half), lambda i, j: (0, 0)),
            pl.BlockSpec((None, d, tn), lambda i, j: (layer, 0, j)),
        ],
        out_specs=pl.BlockSpec((tm, tn), lambda i, j: (i, j)),
        out_shape=jax.ShapeDtypeStruct((t, n), BF16),
        scratch_shapes=[pltpu.VMEM((tm, d), BF16), pltpu.VMEM((tm, half), F32), pltpu.VMEM((tm, half), F32)],
        compiler_params=_params("arbitrary", "arbitrary"),
        name="ret_in",
    )(x, mods, mods, norm_g.reshape(1, d), pos, inv, w)


def _ret_core_kernel(lg_ref, q_ref, k_ref, v_ref, g_ref, gn_ref, o_ref, state_ref, dmask_ref, *, span, hg, dk, dv):
    i = pl.program_id(2)

    @pl.when(i == 0)
    def _():
        state_ref[...] = jnp.zeros_like(state_ref)
        r = lax.broadcasted_iota(jnp.int32, (span, span), 0)
        c = lax.broadcasted_iota(jnp.int32, (span, span), 1)
        dist = jnp.abs(r - c).astype(F32)
        causal = (c // CHUNK) <= (r // CHUNK)
        for hh in range(hg):
            dmask_ref[hh] = jnp.where(causal, jnp.exp(lg_ref[hh][:, :1] * dist), 0.0)

    idx = lax.broadcasted_iota(jnp.int32, (span, 1), 0).astype(F32)
    for hh in range(hg):
        lg = lg_ref[hh][:, :1]
        xi = jnp.exp(lg * (idx + 1.0))
        zeta = jnp.exp(lg * (span - 1.0 - idx))
        g_span = jnp.exp(lg * float(span))

        q = q_ref[:, hh * dk:(hh + 1) * dk]
        k = k_ref[:, hh * dk:(hh + 1) * dk]
        v = v_ref[:, hh * dv:(hh + 1) * dv]
        s = lax.dot_general(q, k, (((1,), (1,)), ((), ())), preferred_element_type=F32) * dmask_ref[hh]
        state = state_ref[hh]
        o = _dot(s.astype(BF16), v) + xi * _dot(q, state.astype(BF16))
        kz = (k.astype(F32) * zeta).astype(BF16)
        state_ref[hh] = state * g_span + lax.dot_general(kz, v, (((0,), (0,)), ((), ())),
                                                         preferred_element_type=F32)

        mu = jnp.mean(o, axis=-1, keepdims=True)
        oc = o - mu
        var = jnp.mean(oc * oc, axis=-1, keepdims=True)
        on = (oc * lax.rsqrt(var + EPS)) * gn_ref[:, hh * dv:(hh + 1) * dv]
        gate = _silu(g_ref[:, hh * dv:(hh + 1) * dv].astype(F32))
        o_ref[:, hh * dv:(hh + 1) * dv] = (gate * on).astype(BF16)


def _ret_core(qkvg, log_g, gn_g, nb, seq, heads, dk, dv, span=512, hg=4):
    t = qkvg.shape[0]
    ns = seq // span
    groups = heads // hg
    k_off = groups
    v_off = (2 * heads * dk) // (hg * dv)
    g_off = v_off + groups
    row = lambda b, h, i: b * ns + i
    return pl.pallas_call(
        functools.partial(_ret_core_kernel, span=span, hg=hg, dk=dk, dv=dv),
        grid=(nb, groups, ns),
        in_specs=[
            pl.BlockSpec((hg, 1, 128), lambda b, h, i: (h, 0, 0)),
            pl.BlockSpec((span, hg * dk), lambda b, h, i: (row(b, h, i), h)),
            pl.BlockSpec((span, hg * dk), lambda b, h, i: (row(b, h, i), k_off + h)),
            pl.BlockSpec((span, hg * dv), lambda b, h, i: (row(b, h, i), v_off + h)),
            pl.BlockSpec((span, hg * dv), lambda b, h, i: (row(b, h, i), g_off + h)),
            pl.BlockSpec((1, hg * dv), lambda b, h, i: (0, h)),
        ],
        out_specs=pl.BlockSpec((span, hg * dv), lambda b, h, i: (row(b, h, i), h)),
        out_shape=jax.ShapeDtypeStruct((t, heads * dv), BF16),
        scratch_shapes=[pltpu.VMEM((hg, dk, dv), F32), pltpu.VMEM((hg, span, span), F32)],
        compiler_params=_params("arbitrary", "arbitrary", "arbitrary"),
        name="ret_core",
    )(log_g, qkvg, qkvg, qkvg, qkvg, gn_g.reshape(1, heads * dv))


def _proj_res_kernel(y_ref, w_ref, x_ref, gt_ref, o_ref):
    o_ref[...] = x_ref[...] + gt_ref[...] * _dot(y_ref[...], w_ref[...])


def _proj_res(y, w, layer, x, mods, mod_idx, nb, tm=1024, tn=512):
    t, kdim = y.shape
    d = w.shape[2]
    bpb = (t // nb) // tm
    return pl.pallas_call(
        _proj_res_kernel,
        grid=(t // tm, d // tn),
        in_specs=[
            pl.BlockSpec((tm, kdim), lambda i, j: (i, 0)),
            pl.BlockSpec((None, kdim, tn), lambda i, j: (layer, 0, j)),
            pl.BlockSpec((tm, tn), lambda i, j: (i, j)),
            pl.BlockSpec((None, 1, tn), lambda i, j: (mod_idx * nb + i // bpb, 0, j)),
        ],
        out_specs=pl.BlockSpec((tm, tn), lambda i, j: (i, j)),
        out_shape=jax.ShapeDtypeStruct((t, d), F32),
        compiler_params=_params("arbitrary", "arbitrary"),
        name="proj_res",
    )(y, w, x, mods)


def _rope_table_kernel(pos_ref, inv_ref, cos_ref, sin_ref):
    cos, sin = _rope_tables(pos_ref, inv_ref)
    cos_ref[...] = cos
    sin_ref[...] = sin


def _mla_rope_tables(positions, half, tr=512):
    t = positions.size
    per_row = 128 // half
    rows = t // per_row
    pos_rep = jnp.repeat(positions.reshape(rows, per_row), half, axis=1)
    inv = ROPE_THETA ** (-jnp.arange(half, dtype=F32) / half)
    inv_rep = jnp.tile(inv, per_row).reshape(1, 128)
    cos, sin = pl.pallas_call(
        _rope_table_kernel,
        grid=(rows // tr,),
        in_specs=[pl.BlockSpec((tr, 128), lambda i: (i, 0)), pl.BlockSpec((1, 128), lambda i: (0, 0))],
        out_specs=[pl.BlockSpec((tr, 128), lambda i: (i, 0))] * 2,
        out_shape=[jax.ShapeDtypeStruct((rows, 128), F32)] * 2,
        compiler_params=_params("arbitrary"),
        name="mla_rope_tables",
    )(pos_rep, inv_rep)
    return cos.reshape(t, half), sin.reshape(t, half)


def _rope_tail(x, cos, sin):
    half = MLA_ROPE // 2
    x1 = x[:, :half]
    x2 = x[:, half:]
    return jnp.concatenate([x1 * cos - x2 * sin, x1 * sin + x2 * cos], axis=-1)


def _kv_kernel(x_ref, sh_ref, sc_ref, g_ref, cos_ref, sin_ref, wdc_ref, wdr_ref, lg_ref, wu_ref, k_ref, v_ref):
    h = _modulated_norm(x_ref[...], g_ref[...], sh_ref[...], sc_ref[...]).astype(BF16)
    ckv = _dot(h, wdc_ref[...])
    ms = jnp.mean(ckv * ckv, axis=-1, keepdims=True)
    ckv = ((ckv * lax.rsqrt(ms + EPS)) * lg_ref[...]).astype(BF16)
    kr = _rope_tail(_dot(h, wdr_ref[...]), cos_ref[...], sin_ref[...]).astype(BF16)
    width = MLA_NOPE + MLA_V
    for hd in range(MLA_HEADS):
        kv = _dot(ckv, wu_ref[:, hd * width:(hd + 1) * width])
        k_ref[hd, :, :MLA_NOPE] = kv[:, :MLA_NOPE].astype(BF16)
        k_ref[hd, :, MLA_NOPE:] = kr
        v_ref[hd] = kv[:, MLA_NOPE:].astype(BF16)


def _shared_kv(x, mods, nb, seq, norm_g, cos, sin, w_dc, w_dr, latent_g, w_ukv, tm=512):
    t, d = x.shape
    spb = seq // tm
    const = lambda i: (0, 0)
    return pl.pallas_call(
        _kv_kernel,
        grid=(t // tm,),
        in_specs=[
            pl.BlockSpec((tm, d), lambda i: (i, 0)),
            _mod_spec(0, nb, spb, d),
            _mod_spec(1, nb, spb, d),
            pl.BlockSpec((1, d), const),
            pl.BlockSpec((tm, MLA_ROPE // 2), lambda i: (i, 0)),
            pl.BlockSpec((tm, MLA_ROPE // 2), lambda i: (i, 0)),
            pl.BlockSpec(w_dc.shape, const),
            pl.BlockSpec(w_dr.shape, const),
            pl.BlockSpec((1, KV_RANK), const),
            pl.BlockSpec(w_ukv.shape, const),
        ],
        out_specs=[
            pl.BlockSpec((None, MLA_HEADS, tm, MLA_NOPE + MLA_ROPE), lambda i: (i // spb, 0, i % spb, 0)),
            pl.BlockSpec((None, MLA_HEADS, tm, MLA_V), lambda i: (i // spb, 0, i % spb, 0)),
        ],
        out_shape=[
            jax.ShapeDtypeStruct((nb, MLA_HEADS, seq, MLA_NOPE + MLA_ROPE), BF16),
            jax.ShapeDtypeStruct((nb, MLA_HEADS, seq, MLA_V), BF16),
        ],
        compiler_params=_params("arbitrary"),
        name="mla_kv",
    )(x, mods, mods, norm_g.reshape(1, d), cos, sin, w_dc, w_dr, latent_g.reshape(1, KV_RANK), w_ukv)


def _q_kernel(x_ref, sh_ref, sc_ref, g_ref, cos_ref, sin_ref, wd_ref, lg_ref, wu_ref, q_ref, *, q_scale):
    h = _modulated_norm(x_ref[...], g_ref[...], sh_ref[...], sc_ref[...]).astype(BF16)
    cq = _dot(h, wd_ref[...])
    ms = jnp.mean(cq * cq, axis=-1, keepdims=True)
    cq = ((cq * lax.rsqrt(ms + EPS)) * lg_ref[...]).astype(BF16)
    cos = cos_ref[...]
    sin = sin_ref[...]
    for hd in range(MLA_HEADS):
        qh = _dot(cq, wu_ref[hd]) * q_scale
        q_ref[hd, :, :MLA_NOPE] = qh[:, :MLA_NOPE].astype(BF16)
        q_ref[hd, :, MLA_NOPE:] = _rope_tail(qh[:, MLA_NOPE:], cos, sin).astype(BF16)


def _mla_q(x, mods, mod_base, nb, seq, norm_g, cos, sin, w_dq, latent_g, w_uq_heads, tm=512):
    t, d = x.shape
    spb = seq // tm
    const = lambda i: (0, 0)
    dq = MLA_NOPE + MLA_ROPE
    return pl.pallas_call(
        functools.partial(_q_kernel, q_scale=dq ** -0.5 * math.log2(math.e)),
        grid=(t // tm,),
        in_specs=[
            pl.BlockSpec((tm, d), lambda i: (i, 0)),
            _mod_spec(mod_base + 0, nb, spb, d),
            _mod_spec(mod_base + 1, nb, spb, d),
            pl.BlockSpec((1, d), const),
            pl.BlockSpec((tm, MLA_ROPE // 2), lambda i: (i, 0)),
            pl.BlockSpec((tm, MLA_ROPE // 2), lambda i: (i, 0)),
            pl.BlockSpec(w_dq.shape, const),
            pl.BlockSpec((1, Q_RANK), const),
            pl.BlockSpec(w_uq_heads.shape, lambda i: (0, 0, 0)),
        ],
        out_specs=pl.BlockSpec((None, MLA_HEADS, tm, dq), lambda i: (i // spb, 0, i % spb, 0)),
        out_shape=jax.ShapeDtypeStruct((nb, MLA_HEADS, seq, dq), BF16),
        compiler_params=_params("arbitrary"),
        name="mla_q",
    )(x, mods, mods, norm_g.reshape(1, d), cos, sin, w_dq, latent_g.reshape(1, Q_RANK), w_uq_heads)


def _attn_kernel(q_ref, k_ref, v_ref, mask_ref, o_ref, *, tq, nq, hg):
    qi = pl.program_id(2)
    nt = (((1,), (1,)), ((), ()))
    visible = mask_ref[...] != 0.0
    dv = v_ref.shape[-1]

    for i in range(nq):
        @pl.when(qi == i)
        def _(i=i):
            for hh in range(hg):
                q = q_ref[hh]
                m = l = acc = None
                for j in range(i, -1, -1):
                    lo = j * tq
                    s = lax.dot_general(q, k_ref[hh, lo:lo + tq, :], nt, preferred_element_type=F32)
                    if j == i:
                        s = jnp.where(visible, s, -1e30)
                    bm = jnp.max(s, axis=-1, keepdims=True)
                    m_new = bm if m is None else jnp.maximum(m, bm)
                    p = jnp.exp2(s - m_new)
                    ps = jnp.sum(p, axis=-1, keepdims=True)
                    pv = _dot(p.astype(BF16), v_ref[hh, lo:lo + tq, :])
                    if m is None:
                        l, acc = ps, pv
                    else:
                        alpha = jnp.exp2(m - m_new)
                        l = alpha * l + ps
                        acc = alpha * acc + pv
                    m = m_new
                o_ref[:, hh * dv:(hh + 1) * dv] = (acc / l).astype(BF16)


def _attention(q, k, v, tq=512, hg=2):
    nb, heads, seq, dq = q.shape
    dv = v.shape[-1]
    nq = seq // tq
    r = jnp.arange(tq, dtype=jnp.int32) // CHUNK
    mask = (r[None, :] <= r[:, None]).astype(F32)
    return pl.pallas_call(
        functools.partial(_attn_kernel, tq=tq, nq=nq, hg=hg),
        grid=(nb, heads // hg, nq),
        in_specs=[
            pl.BlockSpec((None, hg, tq, dq), lambda b, h, i: (b, h, i, 0)),
            pl.BlockSpec((None, hg, seq, dq), lambda b, h, i: (b, h, 0, 0)),
            pl.BlockSpec((None, hg, seq, dv), lambda b, h, i: (b, h, 0, 0)),
            pl.BlockSpec((tq, tq), lambda b, h, i: (0, 0)),
        ],
        out_specs=pl.BlockSpec((tq, hg * dv), lambda b, h, i: (b * nq + i, h)),
        out_shape=jax.ShapeDtypeStruct((nb * seq, heads * dv), BF16),
        compiler_params=_params("arbitrary", "arbitrary", "arbitrary"),
        name="mla_attn",
    )(q, k, v, mask)


def kernel(x, c, positions, ada_w, ada_b, norm_g, ffn_w_in, ffn_w_out, ret_w_in, ret_gn_g, ret_w_out, kv_ada_w, kv_ada_b, kv_norm_g, mla_w_dkv, kv_latent_g, mla_w_ukv, mla_w_dq, q_latent_g, mla_w_uq, mla_w_out, final_g):
    nb, seq, d = x.shape
    t = nb * seq
    depth = ada_w.shape[0]
    n_ret = ret_w_in.shape[0]
    ret_qk = d
    ret_dk = ret_qk // RET_HEADS
    ret_dv = ret_w_out.shape[1] // RET_HEADS

    xs = x.reshape(t, d)
    pos = positions.reshape(t, 1)
    c_pad = jnp.zeros((8, d), F32).at[:nb].set(c)

    half_ret = ret_dk // 2
    inv_ret = (ROPE_THETA ** (-jnp.arange(half_ret, dtype=F32) / half_ret)).reshape(1, half_ret)
    mla_cos, mla_sin = _mla_rope_tables(positions, MLA_ROPE // 2)
    log_g = jnp.log1p(-(2.0 ** (-5.0 - jnp.arange(RET_HEADS, dtype=F32))))
    log_g = jnp.broadcast_to(log_g[:, None, None], (RET_HEADS, 1, 128))

    ffn_order = [(l, k) for l in range(depth) for k in range(2)]
    ffn_w = (ffn_w_in[0, 0].astype(BF16), ffn_w_out[0, 0].astype(BF16))

    def ffn(xs, mods, mod_base, l, k, ffn_w):
        n = ffn_order.index((l, k))
        nxt = (ffn_w_in, ffn_w_out) + ffn_order[n + 1] if n + 1 < len(ffn_order) else None
        return _ffn(xs, mods, mod_base, nb, norm_g[l, 2 * k], ffn_w[0], ffn_w[1], final_g,
                    n + 1 == len(ffn_order), nxt)

    ret_in_w = ret_w_in.astype(BF16)
    ret_out_w = ret_w_out.astype(BF16)
    mla_out_w = mla_w_out.astype(BF16)

    kq = kk = vv = None
    for l in range(depth):
        mods = _mod_table(_mods(c_pad, ada_w, ada_b, l), nb, d)
        if l == n_ret:
            kv_mods = _mod_table(_mods(c_pad, kv_ada_w[None], kv_ada_b[None], 0), nb, d)
            w_dkv = mla_w_dkv.astype(BF16)
            kk, vv = _shared_kv(xs, kv_mods, nb, seq, kv_norm_g, mla_cos, mla_sin,
                                w_dkv[:, :KV_RANK], w_dkv[:, KV_RANK:], kv_latent_g, mla_w_ukv.astype(BF16))
        xs, ffn_w = ffn(xs, mods, 0, l, 0, ffn_w)
        if l < n_ret:
            qkvg = _ret_in(xs, mods, 3, nb, norm_g[l, 1], pos, inv_ret, ret_in_w, l, ret_qk, ret_dk)
            y = _ret_core(qkvg, log_g, ret_gn_g[l], nb, seq, RET_HEADS, ret_dk, ret_dv)
            xs = _proj_res(y, ret_out_w, l, xs, mods, 5, nb)
        else:
            jj = l - n_ret
            dq = MLA_NOPE + MLA_ROPE
            w_uq = mla_w_uq[jj].astype(BF16).reshape(Q_RANK, MLA_HEADS, dq).transpose(1, 0, 2)
            kq = _mla_q(xs, mods, 3, nb, seq, norm_g[l, 1], mla_cos, mla_sin, mla_w_dq[jj].astype(BF16),
                        q_latent_g[jj], w_uq)
            y = _attention(kq, kk, vv)
            xs = _proj_res(y, mla_out_w, jj, xs, mods, 5, nb)
        xs, ffn_w = ffn(xs, mods, 6, l, 1, ffn_w)
    return xs.reshape(nb, seq, d)
```

```python
import functools
import math

import jax
import jax.numpy as jnp
from jax import lax
from jax.experimental import pallas as pl
from jax.experimental.pallas import tpu as pltpu

CHUNK = 64
RET_HEADS = 8
MLA_HEADS = 16
MLA_NOPE = 128
MLA_ROPE = 64
MLA_V = 128
Q_RANK = 512
KV_RANK = 512
ROPE_THETA = 10000.0
EPS = 1e-6
N_MOD = 9

BF16 = jnp.bfloat16
F32 = jnp.float32

VMEM_LIMIT_BYTES = 56 * 1024 * 1024


def _params(*sem):
    return pltpu.CompilerParams(dimension_semantics=sem, vmem_limit_bytes=VMEM_LIMIT_BYTES)


def _silu(x):
    return x / (1.0 + jnp.exp(-x))


def _dot(a, b):
    return jnp.dot(a, b, preferred_element_type=F32)


def _modulated_norm(x, g, shift, scale):
    ms = jnp.mean(x * x, axis=-1, keepdims=True)
    return (x * lax.rsqrt(ms + EPS)) * g * (1.0 + scale) + shift


def _rope_tables(pos_ref, inv_ref):
    ang = pos_ref[...].astype(F32) * inv_ref[...]
    return jnp.cos(ang), jnp.sin(ang)


def _mod_spec(k, nb, blocks_per_batch, d):
    return pl.BlockSpec((None, 1, d), lambda i, *_: (k * nb + i // blocks_per_batch, 0, 0))


def _mods_kernel(c_ref, w_ref, b_ref, o_ref):
    ca = _silu(c_ref[...]).astype(BF16)
    o_ref[...] = _dot(ca, w_ref[...].astype(BF16)) + b_ref[...]


def _mods(c_pad, w, b, layer, tn=1024):
    rows, d = c_pad.shape
    nl, _, n = w.shape
    return pl.pallas_call(
        _mods_kernel,
        grid=(n // tn,),
        in_specs=[
            pl.BlockSpec((rows, d), lambda j: (0, 0)),
            pl.BlockSpec((None, d, tn), lambda j: (layer, 0, j)),
            pl.BlockSpec((None, 1, tn), lambda j: (layer, 0, j)),
        ],
        out_specs=pl.BlockSpec((rows, tn), lambda j: (0, j)),
        out_shape=jax.ShapeDtypeStruct((rows, n), F32),
        compiler_params=_params("arbitrary"),
        name="ada_mods",
    )(c_pad, w, b.reshape(nl, 1, n))


def _mod_table(m, nb, d):
    k = m.shape[1] // d
    return m[:nb].reshape(nb, k, d).transpose(1, 0, 2).reshape(k * nb, 1, d)


def _ffn_kernel(x_ref, sh_ref, sc_ref, gt_ref, g_ref, wg_ref, wu_ref, wo_ref, fg_ref, *rest,
                final_norm, row_splits, cast_next):
    if cast_next:
        nin_ref, nout_ref, o_ref, nin_bf_ref, nout_bf_ref, h_ref, inv_ref = rest
        nin_bf_ref[...] = nin_ref[...].astype(BF16)
        nout_bf_ref[...] = nout_ref[...].astype(BF16)
    else:
        o_ref, h_ref, inv_ref = rest
    j = pl.program_id(1)
    last = pl.num_programs(1) - 1
    rs = x_ref.shape[0] // row_splits
    slabs = [slice(r * rs, (r + 1) * rs) for r in range(row_splits)]

    def step(first):
        for rows in slabs:
            h = h_ref[rows, :]
            gate = _dot(h, wg_ref[...])
            up = _dot(h, wu_ref[...])
            a = (_silu(gate) * up).astype(BF16)
            y = _dot(a, wo_ref[...])
            if first:
                o_ref[rows, :] = y
            else:
                o_ref[rows, :] += y

    @pl.when(j == 0)
    def _():
        for rows in slabs:
            x = x_ref[rows, :]
            inv_ref[rows, :] = lax.rsqrt(jnp.mean(x * x, axis=-1, keepdims=True) + EPS)
            h = (x_ref[rows, :] * inv_ref[rows, :]) * g_ref[...] * (1.0 + sc_ref[...]) + sh_ref[...]
            h_ref[rows, :] = h.astype(BF16)
        step(True)

    @pl.when(jnp.logical_and(j > 0, j < last))
    def _():
        step(False)

    @pl.when(j == last)
    def _():
        step(False)
        for rows in slabs:
            y = x_ref[rows, :] + (0.5 * gt_ref[...]) * o_ref[rows, :]
            o_ref[rows, :] = y
            if final_norm:
                inv_ref[rows, :] = lax.rsqrt(jnp.mean(y * y, axis=-1, keepdims=True) + EPS)
                o_ref[rows, :] = (o_ref[rows, :] * inv_ref[rows, :]) * fg_ref[...]


def _ffn(x, mods, mod_base, nb, norm_g, w_in, w_out, final_g, final_norm, next_w=None, tm=1024, tf=512):
    t, d = x.shape
    dff = w_out.shape[0]
    nf = dff // tf
    assert nf >= 2, "the first and the last ff step are distinct branches"
    ni = t // tm
    bpb = (t // nb) // tm
    in_specs = [
        pl.BlockSpec((tm, d), lambda i, j: (i, 0)),
        _mod_spec(mod_base + 0, nb, bpb, d),
        _mod_spec(mod_base + 1, nb, bpb, d),
        _mod_spec(mod_base + 2, nb, bpb, d),
        pl.BlockSpec((1, d), lambda i, j: (0, 0)),
        pl.BlockSpec((d, tf), lambda i, j: (0, j)),
        pl.BlockSpec((d, tf), lambda i, j: (0, nf + j)),
        pl.BlockSpec((tf, d), lambda i, j: (j, 0)),
        pl.BlockSpec((1, d), lambda i, j: (0, 0)),
    ]
    args = [x, mods, mods, mods, norm_g.reshape(1, d), w_in, w_in, w_out, final_g.reshape(1, d)]
    out_specs = [pl.BlockSpec((tm, d), lambda i, j: (i, 0))]
    out_shape = [jax.ShapeDtypeStruct((t, d), F32)]
    if next_w is not None:
        nin, nout, nl, nk = next_w
        in_blk = (d // ni, 2 * dff // nf)
        out_blk = (dff // nf, d // ni)
        in_specs += [pl.BlockSpec((None, None) + in_blk, lambda i, j: (nl, nk, i, j)),
                     pl.BlockSpec((None, None) + out_blk, lambda i, j: (nl, nk, j, i))]
        args += [nin, nout]
        out_specs += [pl.BlockSpec(in_blk, lambda i, j: (i, j)), pl.BlockSpec(out_blk, lambda i, j: (j, i))]
        out_shape += [jax.ShapeDtypeStruct((d, 2 * dff), BF16), jax.ShapeDtypeStruct((dff, d), BF16)]
    res = pl.pallas_call(
        functools.partial(_ffn_kernel, final_norm=final_norm, row_splits=4, cast_next=next_w is not None),
        grid=(ni, nf),
        in_specs=in_specs,
        out_specs=out_specs,
        out_shape=out_shape,
        scratch_shapes=[pltpu.VMEM((tm, d), BF16), pltpu.VMEM((tm, 1), F32)],
        compiler_params=_params("arbitrary", "arbitrary"),
        name="ffn",
    )(*args)
    return res[0], (tuple(res[1:]) if next_w is not None else None)


def _ret_in_kernel(x_ref, sh_ref, sc_ref, g_ref, pos_ref, inv_ref, w_ref, o_ref, h_ref, cos_ref, sin_ref,
                   *, n_q_tiles, dk, q_scale):
    j = pl.program_id(1)

    @pl.when(j == 0)
    def _():
        h = _modulated_norm(x_ref[...], g_ref[...], sh_ref[...], sc_ref[...])
        h_ref[...] = h.astype(BF16)
        cos, sin = _rope_tables(pos_ref, inv_ref)
        cos_ref[...] = cos
        sin_ref[...] = sin

    y = _dot(h_ref[...], w_ref[...])
    tn = y.shape[1]
    half = dk // 2

    def roped(scale):
        cos = cos_ref[...]
        sin = sin_ref[...]
        for hd in range(tn // dk):
            x1 = y[:, hd * dk:hd * dk + half]
            x2 = y[:, hd * dk + half:(hd + 1) * dk]
            o_ref[:, hd * dk:hd * dk + half] = ((x1 * cos - x2 * sin) * scale).astype(BF16)
            o_ref[:, hd * dk + half:(hd + 1) * dk] = ((x1 * sin + x2 * cos) * scale).astype(BF16)

    @pl.when(j < n_q_tiles)
    def _():
        roped(q_scale)

    @pl.when(jnp.logical_and(j >= n_q_tiles, j < 2 * n_q_tiles))
    def _():
        roped(1.0)

    @pl.when(j >= 2 * n_q_tiles)
    def _():
        o_ref[...] = y.astype(BF16)


def _ret_in(x, mods, mod_base, nb, norm_g, pos, inv, w, layer, qk_dim, dk, tm=1024, tn=1024):
    t, d = x.shape
    n = w.shape[2]
    bpb = (t // nb) // tm
    half = dk // 2
    return pl.pallas_call(
        functools.partial(_ret_in_kernel, n_q_tiles=qk_dim // tn, dk=dk, q_scale=dk ** -0.5),
        grid=(t // tm, n // tn),
        in_specs=[
            pl.BlockSpec((tm, d), lambda i, j: (i, 0)),
            _mod_spec(mod_base + 0, nb, bpb, d),
            _mod_spec(mod_base + 1, nb, bpb, d),
            pl.BlockSpec((1, d), lambda i, j: (0, 0)),
            pl.BlockSpec((tm, 1), lambda i, j: (i, 0)),
            pl.BlockSpec((1, half), lambda i, j: (0, 0)),
            pl.BlockSpec((None, d, tn), lambda i, j: (layer, 0, j)),
        ],
        out_specs=pl.BlockSpec((tm, tn), lambda i, j: (i, j)),
        out_shape=jax.ShapeDtypeStruct((t, n), BF16),
        scratch_shapes=[pltpu.VMEM((tm, d), BF16), pltpu.VMEM((tm, half), F32), pltpu.VMEM((tm, half), F32)],
        compiler_params=_params("arbitrary", "arbitrary"),
        name="ret_in",
    )(x, mods, mods, norm_g.reshape(1, d), pos, inv, w)


def _ret_core_kernel(lg_ref, q_ref, k_ref, v_ref, g_ref, gn_ref, o_ref, state_ref, dmask_ref, *, span, hg, dk, dv):
    i = pl.program_id(2)

    @pl.when(i == 0)
    def _():
        state_ref[...] = jnp.zeros_like(state_ref)
        r = lax.broadcasted_iota(jnp.int32, (span, span), 0)
        c = lax.broadcasted_iota(jnp.int32, (span, span), 1)
        dist = jnp.abs(r - c).astype(F32)
        causal = (c // CHUNK) <= (r // CHUNK)
        for hh in range(hg):
            dmask_ref[hh] = jnp.where(causal, jnp.exp(lg_ref[hh][:, :1] * dist), 0.0)

    idx = lax.broadcasted_iota(jnp.int32, (span, 1), 0).astype(F32)
    for hh in range(hg):
        lg = lg_ref[hh][:, :1]
        xi = jnp.exp(lg * (idx + 1.0))
        zeta = jnp.exp(lg * (span - 1.0 - idx))
        g_span = jnp.exp(lg * float(span))

        q = q_ref[:, hh * dk:(hh + 1) * dk]
        k = k_ref[:, hh * dk:(hh + 1) * dk]
        v = v_ref[:, hh * dv:(hh + 1) * dv]
        s = lax.dot_general(q, k, (((1,), (1,)), ((), ())), preferred_element_type=F32) * dmask_ref[hh]
        state = state_ref[hh]
        o = _dot(s.astype(BF16), v) + xi * _dot(q, state.astype(BF16))
        kz = (k.astype(F32) * zeta).astype(BF16)
        state_ref[hh] = state * g_span + lax.dot_general(kz, v, (((0,), (0,)), ((), ())),
                                                         preferred_element_type=F32)

        mu = jnp.mean(o, axis=-1, keepdims=True)
        oc = o - mu
        var = jnp.mean(oc * oc, axis=-1, keepdims=True)
        on = (oc * lax.rsqrt(var + EPS)) * gn_ref[:, hh * dv:(hh + 1) * dv]
        gate = _silu(g_ref[:, hh * dv:(hh + 1) * dv].astype(F32))
        o_ref[:, hh * dv:(hh + 1) * dv] = (gate * on).astype(BF16)


def _ret_core(qkvg, log_g, gn_g, nb, seq, heads, dk, dv, span=512, hg=4):
    t = qkvg.shape[0]
    ns = seq // span
    groups = heads // hg
    k_off = groups
    v_off = (2 * heads * dk) // (hg * dv)
    g_off = v_off + groups
    row = lambda b, h, i: b * ns + i
    return pl.pallas_call(
        functools.partial(_ret_core_kernel, span=span, hg=hg, dk=dk, dv=dv),
        grid=(nb, groups, ns),
        in_specs=[
            pl.BlockSpec((hg, 1, 128), lambda b, h, i: (h, 0, 0)),
            pl.BlockSpec((span, hg * dk), lambda b, h, i: (row(b, h, i), h)),
            pl.BlockSpec((span, hg * dk), lambda b, h, i: (row(b, h, i), k_off + h)),
            pl.BlockSpec((span, hg * dv), lambda b, h, i: (row(b, h, i), v_off + h)),
            pl.BlockSpec((span, hg * dv), lambda b, h, i: (row(b, h, i), g_off + h)),
            pl.BlockSpec((1, hg * dv), lambda b, h, i: (0, h)),
        ],
        out_specs=pl.BlockSpec((span, hg * dv), lambda b, h, i: (row(b, h, i), h)),
        out_shape=jax.ShapeDtypeStruct((t, heads * dv), BF16),
        scratch_shapes=[pltpu.VMEM((hg, dk, dv), F32), pltpu.VMEM((hg, span, span), F32)],
        compiler_params=_params("arbitrary", "arbitrary", "arbitrary"),
        name="ret_core",
    )(log_g, qkvg, qkvg, qkvg, qkvg, gn_g.reshape(1, heads * dv))


def _proj_res_kernel(y_ref, w_ref, x_ref, gt_ref, o_ref):
    o_ref[...] = x_ref[...] + gt_ref[...] * _dot(y_ref[...], w_ref[...])


def _proj_res(y, w, layer, x, mods, mod_idx, nb, tm=1024, tn=512):
    t, kdim = y.shape
    d = w.shape[2]
    bpb = (t // nb) // tm
    return pl.pallas_call(
        _proj_res_kernel,
        grid=(t // tm, d // tn),
        in_specs=[
            pl.BlockSpec((tm, kdim), lambda i, j: (i, 0)),
            pl.BlockSpec((None, kdim, tn), lambda i, j: (layer, 0, j)),
            pl.BlockSpec((tm, tn), lambda i, j: (i, j)),
            pl.BlockSpec((None, 1, tn), lambda i, j: (mod_idx * nb + i // bpb, 0, j)),
        ],
        out_specs=pl.BlockSpec((tm, tn), lambda i, j: (i, j)),
        out_shape=jax.ShapeDtypeStruct((t, d), F32),
        compiler_params=_params("arbitrary", "arbitrary"),
        name="proj_res",
    )(y, w, x, mods)


def _rope_table_kernel(pos_ref, inv_ref, cos_ref, sin_ref):
    cos, sin = _rope_tables(pos_ref, inv_ref)
    cos_ref[...] = cos
    sin_ref[...] = sin


def _mla_rope_tables(positions, half, tr=512):
    t = positions.size
    per_row = 128 // half
    rows = t // per_row
    pos_rep = jnp.repeat(positions.reshape(rows, per_row), half, axis=1)
    inv = ROPE_THETA ** (-jnp.arange(half, dtype=F32) / half)
    inv_rep = jnp.tile(inv, per_row).reshape(1, 128)
    cos, sin = pl.pallas_call(
        _rope_table_kernel,
        grid=(rows // tr,),
        in_specs=[pl.BlockSpec((tr, 128), lambda i: (i, 0)), pl.BlockSpec((1, 128), lambda i: (0, 0))],
        out_specs=[pl.BlockSpec((tr, 128), lambda i: (i, 0))] * 2,
        out_shape=[jax.ShapeDtypeStruct((rows, 128), F32)] * 2,
        compiler_params=_params("arbitrary"),
        name="mla_rope_tables",
    )(pos_rep, inv_rep)
    return cos.reshape(t, half), sin.reshape(t, half)


def _rope_tail(x, cos, sin):
    half = MLA_ROPE // 2
    x1 = x[:, :half]
    x2 = x[:, half:]
    return jnp.concatenate([x1 * cos - x2 * sin, x1 * sin + x2 * cos], axis=-1)


def _kv_kernel(x_ref, sh_ref, sc_ref, g_ref, cos_ref, sin_ref, wdc_ref, wdr_ref, lg_ref, wu_ref, k_ref, v_ref):
    h = _modulated_norm(x_ref[...], g_ref[...], sh_ref[...], sc_ref[...]).astype(BF16)
    ckv = _dot(h, wdc_ref[...])
    ms = jnp.mean(ckv * ckv, axis=-1, keepdims=True)
    ckv = ((ckv * lax.rsqrt(ms + EPS)) * lg_ref[...]).astype(BF16)
    kr = _rope_tail(_dot(h, wdr_ref[...]), cos_ref[...], sin_ref[...]).astype(BF16)
    width = MLA_NOPE + MLA_V
    for hd in range(MLA_HEADS):
        kv = _dot(ckv, wu_ref[:, hd * width:(hd + 1) * width])
        k_ref[hd, :, :MLA_NOPE] = kv[:, :MLA_NOPE].astype(BF16)
        k_ref[hd, :, MLA_NOPE:] = kr
        v_ref[hd] = kv[:, MLA_NOPE:].astype(BF16)


def _shared_kv(x, mods, nb, seq, norm_g, cos, sin, w_dc, w_dr, latent_g, w_ukv, tm=512):
    t, d = x.shape
    spb = seq // tm
    const = lambda i: (0, 0)
    return pl.pallas_call(
        _kv_kernel,
        grid=(t // tm,),
        in_specs=[
            pl.BlockSpec((tm, d), lambda i: (i, 0)),
            _mod_spec(0, nb, spb, d),
            _mod_spec(1, nb, spb, d),
            pl.BlockSpec((1, d), const),
            pl.BlockSpec((tm, MLA_ROPE // 2), lambda i: (i, 0)),
            pl.BlockSpec((tm, MLA_ROPE // 2), lambda i: (i, 0)),
            pl.BlockSpec(w_dc.shape, const),
            pl.BlockSpec(w_dr.shape, const),
            pl.BlockSpec((1, KV_RANK), const),
            pl.BlockSpec(w_ukv.shape, const),
        ],
        out_specs=[
            pl.BlockSpec((None, MLA_HEADS, tm, MLA_NOPE + MLA_ROPE), lambda i: (i // spb, 0, i % spb, 0)),
            pl.BlockSpec((None, MLA_HEADS, tm, MLA_V), lambda i: (i // spb, 0, i % spb, 0)),
        ],
        out_shape=[
            jax.ShapeDtypeStruct((nb, MLA_HEADS, seq, MLA_NOPE + MLA_ROPE), BF16),
            jax.ShapeDtypeStruct((nb, MLA_HEADS, seq, MLA_V), BF16),
        ],
        compiler_params=_params("arbitrary"),
        name="mla_kv",
    )(x, mods, mods, norm_g.reshape(1, d), cos, sin, w_dc, w_dr, latent_g.reshape(1, KV_RANK), w_ukv)


def _q_kernel(x_ref, sh_ref, sc_ref, g_ref, cos_ref, sin_ref, wd_ref, lg_ref, wu_ref, q_ref, *, q_scale):
    h = _modulated_norm(x_ref[...], g_ref[...], sh_ref[...], sc_ref[...]).astype(BF16)
    cq = _dot(h, wd_ref[...])
    ms = jnp.mean(cq * cq, axis=-1, keepdims=True)
    cq = ((cq * lax.rsqrt(ms + EPS)) * lg_ref[...]).astype(BF16)
    cos = cos_ref[...]
    sin = sin_ref[...]
    for hd in range(MLA_HEADS):
        qh = _dot(cq, wu_ref[hd]) * q_scale
        q_ref[hd, :, :MLA_NOPE] = qh[:, :MLA_NOPE].astype(BF16)
        q_ref[hd, :, MLA_NOPE:] = _rope_tail(qh[:, MLA_NOPE:], cos, sin).astype(BF16)


def _mla_q(x, mods, mod_base, nb, seq, norm_g, cos, sin, w_dq, latent_g, w_uq_heads, tm=512):
    t, d = x.shape
    spb = seq // tm
    const = lambda i: (0, 0)
    dq = MLA_NOPE + MLA_ROPE
    return pl.pallas_call(
        functools.partial(_q_kernel, q_scale=dq ** -0.5 * math.log2(math.e)),
        grid=(t // tm,),
        in_specs=[
            pl.BlockSpec((tm, d), lambda i: (i, 0)),
            _mod_spec(mod_base + 0, nb, spb, d),
            _mod_spec(mod_base + 1, nb, spb, d),
            pl.BlockSpec((1, d), const),
            pl.BlockSpec((tm, MLA_ROPE // 2), lambda i: (i, 0)),
            pl.BlockSpec((tm, MLA_ROPE // 2), lambda i: (i, 0)),
            pl.BlockSpec(w_dq.shape, const),
            pl.BlockSpec((1, Q_RANK), const),
            pl.BlockSpec(w_uq_heads.shape, lambda i: (0, 0, 0)),
        ],
        out_specs=pl.BlockSpec((None, MLA_HEADS, tm, dq), lambda i: (i // spb, 0, i % spb, 0)),
        out_shape=jax.ShapeDtypeStruct((nb, MLA_HEADS, seq, dq), BF16),
        compiler_params=_params("arbitrary"),
        name="mla_q",
    )(x, mods, mods, norm_g.reshape(1, d), cos, sin, w_dq, latent_g.reshape(1, Q_RANK), w_uq_heads)


def _attn_kernel(q_ref, k_ref, v_ref, mask_ref, o_ref, *, tq, nq, hg):
    qi = pl.program_id(2)
    nt = (((1,), (1,)), ((), ()))
    visible = mask_ref[...] != 0.0
    dv = v_ref.shape[-1]

    for i in range(nq):
        @pl.when(qi == i)
        def _(i=i):
            for hh in range(hg):
                q = q_ref[hh]
                m = l = acc = None
                for j in range(i, -1, -1):
                    lo = j * tq
                    s = lax.dot_general(q, k_ref[hh, lo:lo + tq, :], nt, preferred_element_type=F32)
                    if j == i:
                        s = jnp.where(visible, s, -1e30)
                    bm = jnp.max(s, axis=-1, keepdims=True)
                    m_new = bm if m is None else jnp.maximum(m, bm)
                    p = jnp.exp2(s - m_new)
                    ps = jnp.sum(p, axis=-1, keepdims=True)
                    pv = _dot(p.astype(BF16), v_ref[hh, lo:lo + tq, :])
                    if m is None:
                        l, acc = ps, pv
                    else:
                        alpha = jnp.exp2(m - m_new)
                        l = alpha * l + ps
                        acc = alpha * acc + pv
                    m = m_new
                o_ref[:, hh * dv:(hh + 1) * dv] = (acc / l).astype(BF16)


def _attention(q, k, v, tq=512, hg=2):
    nb, heads, seq, dq = q.shape
    dv = v.shape[-1]
    nq = seq // tq
    r = jnp.arange(tq, dtype=jnp.int32) // CHUNK
    mask = (r[None, :] <= r[:, None]).astype(F32)
    return pl.pallas_call(
        functools.partial(_attn_kernel, tq=tq, nq=nq, hg=hg),
        grid=(nb, heads // hg, nq),
        in_specs=[
            pl.BlockSpec((None, hg, tq, dq), lambda b, h, i: (b, h, i, 0)),
            pl.BlockSpec((None, hg, seq, dq), lambda b, h, i: (b, h, 0, 0)),
            pl.BlockSpec((None, hg, seq, dv), lambda b, h, i: (b, h, 0, 0)),
            pl.BlockSpec((tq, tq), lambda b, h, i: (0, 0)),
        ],
        out_specs=pl.BlockSpec((tq, hg * dv), lambda b, h, i: (b * nq + i, h)),
        out_shape=jax.ShapeDtypeStruct((nb * seq, heads * dv), BF16),
        compiler_params=_params("arbitrary", "arbitrary", "arbitrary"),
        name="mla_attn",
    )(q, k, v, mask)


def kernel(x, c, positions, ada_w, ada_b, norm_g, ffn_w_in, ffn_w_out, ret_w_in, ret_gn_g, ret_w_out, kv_ada_w, kv_ada_b, kv_norm_g, mla_w_dkv, kv_latent_g, mla_w_ukv, mla_w_dq, q_latent_g, mla_w_uq, mla_w_out, final_g):
    nb, seq, d = x.shape
    t = nb * seq
    depth = ada_w.shape[0]
    n_ret = ret_w_in.shape[0]
    ret_qk = d
    ret_dk = ret_qk // RET_HEADS
    ret_dv = ret_w_out.shape[1] // RET_HEADS

    xs = x.reshape(t, d)
    pos = positions.reshape(t, 1)
    c_pad = jnp.zeros((8, d), F32).at[:nb].set(c)

    half_ret = ret_dk // 2
    inv_ret = (ROPE_THETA ** (-jnp.arange(half_ret, dtype=F32) / half_ret)).reshape(1, half_ret)
    mla_cos, mla_sin = _mla_rope_tables(positions, MLA_ROPE // 2)
    log_g = jnp.log1p(-(2.0 ** (-5.0 - jnp.arange(RET_HEADS, dtype=F32))))
    log_g = jnp.broadcast_to(log_g[:, None, None], (RET_HEADS, 1, 128))

    ffn_order = [(l, k) for l in range(depth) for k in range(2)]
    ffn_w = (ffn_w_in[0, 0].astype(BF16), ffn_w_out[0, 0].astype(BF16))

    def ffn(xs, mods, mod_base, l, k, ffn_w):
        n = ffn_order.index((l, k))
        nxt = (ffn_w_in, ffn_w_out) + ffn_order[n + 1] if n + 1 < len(ffn_order) else None
        return _ffn(xs, mods, mod_base, nb, norm_g[l, 2 * k], ffn_w[0], ffn_w[1], final_g,
                    n + 1 == len(ffn_order), nxt)

    ret_in_w = ret_w_in.astype(BF16)
    ret_out_w = ret_w_out.astype(BF16)
    mla_out_w = mla_w_out.astype(BF16)

    kq = kk = vv = None
    for l in range(depth):
        mods = _mod_table(_mods(c_pad, ada_w, ada_b, l), nb, d)
        if l == n_ret:
            kv_mods = _mod_table(_mods(c_pad, kv_ada_w[None], kv_ada_b[None], 0), nb, d)
            w_dkv = mla_w_dkv.astype(BF16)
            kk, vv = _shared_kv(xs, kv_mods, nb, seq, kv_norm_g, mla_cos, mla_sin,
                                w_dkv[:, :KV_RANK], w_dkv[:, KV_RANK:], kv_latent_g, mla_w_ukv.astype(BF16))
        xs, ffn_w = ffn(xs, mods, 0, l, 0, ffn_w)
        if l < n_ret:
            qkvg = _ret_in(xs, mods, 3, nb, norm_g[l, 1], pos, inv_ret, ret_in_w, l, ret_qk, ret_dk)
            y = _ret_core(qkvg, log_g, ret_gn_g[l], nb, seq, RET_HEADS, ret_dk, ret_dv)
            xs = _proj_res(y, ret_out_w, l, xs, mods, 5, nb)
        else:
            jj = l - n_ret
            dq = MLA_NOPE + MLA_ROPE
            w_uq = mla_w_uq[jj].astype(BF16).reshape(Q_RANK, MLA_HEADS, dq).transpose(1, 0, 2)
            kq = _mla_q(xs, mods, 3, nb, seq, norm_g[l, 1], mla_cos, mla_sin, mla_w_dq[jj].astype(BF16),
                        q_latent_g[jj], w_uq)
            y = _attention(kq, kk, vv)
            xs = _proj_res(y, mla_out_w, jj, xs, mods, 5, nb, tn=1024)
        xs, ffn_w = ffn(xs, mods, 6, l, 1, ffn_w)
    return xs.reshape(nb, seq, d)
```

```python
import functools
import math

import jax
import jax.numpy as jnp
from jax import lax
from jax.experimental import pallas as pl
from jax.experimental.pallas import tpu as pltpu

CHUNK = 64
RET_HEADS = 8
MLA_HEADS = 16
MLA_NOPE = 128
MLA_ROPE = 64
MLA_V = 128
Q_RANK = 512
KV_RANK = 512
ROPE_THETA = 10000.0
EPS = 1e-6
N_MOD = 9

BF16 = jnp.bfloat16
F32 = jnp.float32

VMEM_LIMIT_BYTES = 56 * 1024 * 1024


def _params(*sem):
    return pltpu.CompilerParams(dimension_semantics=sem, vmem_limit_bytes=VMEM_LIMIT_BYTES)


def _silu(x):
    return x / (1.0 + jnp.exp(-x))


def _dot(a, b):
    return jnp.dot(a, b, preferred_element_type=F32)


def _modulated_norm(x, g, shift, scale):
    ms = jnp.mean(x * x, axis=-1, keepdims=True)
    return (x * lax.rsqrt(ms + EPS)) * g * (1.0 + scale) + shift


def _rope_tables(pos_ref, inv_ref):
    ang = pos_ref[...].astype(F32) * inv_ref[...]
    return jnp.cos(ang), jnp.sin(ang)


def _mod_spec(k, nb, blocks_per_batch, d):
    return pl.BlockSpec((None, 1, d), lambda i, *_: (k * nb + i // blocks_per_batch, 0, 0))


def _mods_kernel(c_ref, w_ref, b_ref, o_ref):
    ca = _silu(c_ref[...]).astype(BF16)
    o_ref[...] = _dot(ca, w_ref[...].astype(BF16)) + b_ref[...]


def _mods(c_pad, w, b, layer, tn=1024):
    rows, d = c_pad.shape
    nl, _, n = w.shape
    return pl.pallas_call(
        _mods_kernel,
        grid=(n // tn,),
        in_specs=[
            pl.BlockSpec((rows, d), lambda j: (0, 0)),
            pl.BlockSpec((None, d, tn), lambda j: (layer, 0, j)),
            pl.BlockSpec((None, 1, tn), lambda j: (layer, 0, j)),
        ],
        out_specs=pl.BlockSpec((rows, tn), lambda j: (0, j)),
        out_shape=jax.ShapeDtypeStruct((rows, n), F32),
        compiler_params=_params("arbitrary"),
        name="ada_mods",
    )(c_pad, w, b.reshape(nl, 1, n))


def _mod_table(m, nb, d):
    k = m.shape[1] // d
    return m[:nb].reshape(nb, k, d).transpose(1, 0, 2).reshape(k * nb, 1, d)


def _ffn_kernel(x_ref, sh_ref, sc_ref, gt_ref, g_ref, wg_ref, wu_ref, wo_ref, fg_ref, *rest,
                final_norm, row_splits, cast_next):
    if cast_next:
        nin_ref, nout_ref, o_ref, nin_bf_ref, nout_bf_ref, h_ref, inv_ref = rest
        nin_bf_ref[...] = nin_ref[...].astype(BF16)
        nout_bf_ref[...] = nout_ref[...].astype(BF16)
    else:
        o_ref, h_ref, inv_ref = rest
    j = pl.program_id(1)
    last = pl.num_programs(1) - 1
    rs = x_ref.shape[0] // row_splits
    slabs = [slice(r * rs, (r + 1) * rs) for r in range(row_splits)]

    def step(first):
        for rows in slabs:
            h = h_ref[rows, :]
            gate = _dot(h, wg_ref[...])
            up = _dot(h, wu_ref[...])
            a = (_silu(gate) * up).astype(BF16)
            y = _dot(a, wo_ref[...])
            if first:
                o_ref[rows, :] = y
            else:
                o_ref[rows, :] += y

    @pl.when(j == 0)
    def _():
        for rows in slabs:
            x = x_ref[rows, :]
            inv_ref[rows, :] = lax.rsqrt(jnp.mean(x * x, axis=-1, keepdims=True) + EPS)
            h = (x_ref[rows, :] * inv_ref[rows, :]) * g_ref[...] * (1.0 + sc_ref[...]) + sh_ref[...]
            h_ref[rows, :] = h.astype(BF16)
        step(True)

    @pl.when(jnp.logical_and(j > 0, j < last))
    def _():
        step(False)

    @pl.when(j == last)
    def _():
        step(False)
        for rows in slabs:
            y = x_ref[rows, :] + (0.5 * gt_ref[...]) * o_ref[rows, :]
            o_ref[rows, :] = y
            if final_norm:
                inv_ref[rows, :] = lax.rsqrt(jnp.mean(y * y, axis=-1, keepdims=True) + EPS)
                o_ref[rows, :] = (o_ref[rows, :] * inv_ref[rows, :]) * fg_ref[...]


def _ffn(x, mods, mod_base, nb, norm_g, w_in, w_out, final_g, final_norm, next_w=None, tm=1024, tf=512):
    t, d = x.shape
    dff = w_out.shape[0]
    nf = dff // tf
    assert nf >= 2, "the first and the last ff step are distinct branches"
    ni = t // tm
    bpb = (t // nb) // tm
    in_specs = [
        pl.BlockSpec((tm, d), lambda i, j: (i, 0)),
        _mod_spec(mod_base + 0, nb, bpb, d),
        _mod_spec(mod_base + 1, nb, bpb, d),
        _mod_spec(mod_base + 2, nb, bpb, d),
        pl.BlockSpec((1, d), lambda i, j: (0, 0)),
        pl.BlockSpec((d, tf), lambda i, j: (0, j)),
        pl.BlockSpec((d, tf), lambda i, j: (0, nf + j)),
        pl.BlockSpec((tf, d), lambda i, j: (j, 0)),
        pl.BlockSpec((1, d), lambda i, j: (0, 0)),
    ]
    args = [x, mods, mods, mods, norm_g.reshape(1, d), w_in, w_in, w_out, final_g.reshape(1, d)]
    out_specs = [pl.BlockSpec((tm, d), lambda i, j: (i, 0))]
    out_shape = [jax.ShapeDtypeStruct((t, d), F32)]
    if next_w is not None:
        nin, nout, nl, nk = next_w
        in_blk = (d // ni, 2 * dff // nf)
        out_blk = (dff // nf, d // ni)
        in_specs += [pl.BlockSpec((None, None) + in_blk, lambda i, j: (nl, nk, i, j)),
                     pl.BlockSpec((None, None) + out_blk, lambda i, j: (nl, nk, j, i))]
        args += [nin, nout]
        out_specs += [pl.BlockSpec(in_blk, lambda i, j: (i, j)), pl.BlockSpec(out_blk, lambda i, j: (j, i))]
        out_shape += [jax.ShapeDtypeStruct((d, 2 * dff), BF16), jax.ShapeDtypeStruct((dff, d), BF16)]
    res = pl.pallas_call(
        functools.partial(_ffn_kernel, final_norm=final_norm, row_splits=4, cast_next=next_w is not None),
        grid=(ni, nf),
        in_specs=in_specs,
        out_specs=out_specs,
        out_shape=out_shape,
        scratch_shapes=[pltpu.VMEM((tm, d), BF16), pltpu.VMEM((tm, 1), F32)],
        compiler_params=_params("arbitrary", "arbitrary"),
        name="ffn",
    )(*args)
    return res[0], (tuple(res[1:]) if next_w is not None else None)


def _ret_in_kernel(x_ref, sh_ref, sc_ref, g_ref, pos_ref, inv_ref, w_ref, o_ref, h_ref, cos_ref, sin_ref,
                   *, n_q_tiles, dk, q_scale, row_splits):
    j = pl.program_id(1)
    rs = x_ref.shape[0] // row_splits
    slabs = [slice(r * rs, (r + 1) * rs) for r in range(row_splits)]
    tn = o_ref.shape[1]
    half = dk // 2

    def project(rows, rope_scale):
        y = _dot(h_ref[rows, :], w_ref[...])
        if rope_scale is None:
            o_ref[rows, :] = y.astype(BF16)
            return
        cos = cos_ref[rows, :]
        sin = sin_ref[rows, :]
        for hd in range(tn // dk):
            x1 = y[:, hd * dk:hd * dk + half]
            x2 = y[:, hd * dk + half:(hd + 1) * dk]
            o_ref[rows, hd * dk:hd * dk + half] = ((x1 * cos - x2 * sin) * rope_scale).astype(BF16)
            o_ref[rows, hd * dk + half:(hd + 1) * dk] = ((x1 * sin + x2 * cos) * rope_scale).astype(BF16)

    @pl.when(j == 0)
    def _():
        for rows in slabs:
            h = _modulated_norm(x_ref[rows, :], g_ref[...], sh_ref[...], sc_ref[...])
            h_ref[rows, :] = h.astype(BF16)
            ang = pos_ref[rows, :].astype(F32) * inv_ref[...]
            cos_ref[rows, :] = jnp.cos(ang)
            sin_ref[rows, :] = jnp.sin(ang)
            project(rows, q_scale)

    @pl.when(jnp.logical_and(j > 0, j < n_q_tiles))
    def _():
        for rows in slabs:
            project(rows, q_scale)

    @pl.when(jnp.logical_and(j >= n_q_tiles, j < 2 * n_q_tiles))
    def _():
        for rows in slabs:
            project(rows, 1.0)

    @pl.when(j >= 2 * n_q_tiles)
    def _():
        for rows in slabs:
            project(rows, None)


def _ret_in(x, mods, mod_base, nb, norm_g, pos, inv, w, layer, qk_dim, dk, tm=1024, tn=1024):
    t, d = x.shape
    n = w.shape[2]
    bpb = (t // nb) // tm
    half = dk // 2
    return pl.pallas_call(
        functools.partial(_ret_in_kernel, n_q_tiles=qk_dim // tn, dk=dk, q_scale=dk ** -0.5, row_splits=4),
        grid=(t // tm, n // tn),
        in_specs=[
            pl.BlockSpec((tm, d), lambda i, j: (i, 0)),
            _mod_spec(mod_base + 0, nb, bpb, d),
            _mod_spec(mod_base + 1, nb, bpb, d),
            pl.BlockSpec((1, d), lambda i, j: (0, 0)),
            pl.BlockSpec((tm, 1), lambda i, j: (i, 0)),
            pl.BlockSpec((1, half), lambda i, j: (0, 0)),
            pl.BlockSpec((None, d, tn), lambda i, j: (layer, 0, j)),
        ],
        out_specs=pl.BlockSpec((tm, tn), lambda i, j: (i, j)),
        out_shape=jax.ShapeDtypeStruct((t, n), BF16),
        scratch_shapes=[pltpu.VMEM((tm, d), BF16), pltpu.VMEM((tm, half), F32), pltpu.VMEM((tm, half), F32)],
        compiler_params=_params("arbitrary", "arbitrary"),
        name="ret_in",
    )(x, mods, mods, norm_g.reshape(1, d), pos, inv, w)


def _ret_core_kernel(lg_ref, q_ref, k_ref, v_ref, g_ref, gn_ref, o_ref, state_ref, dmask_ref, *, span, hg, dk, dv):
    i = pl.program_id(2)

    @pl.when(i == 0)
    def _():
        state_ref[...] = jnp.zeros_like(state_ref)
        r = lax.broadcasted_iota(jnp.int32, (span, span), 0)
        c = lax.broadcasted_iota(jnp.int32, (span, span), 1)
        dist = jnp.abs(r - c).astype(F32)
        causal = (c // CHUNK) <= (r // CHUNK)
        for hh in range(hg):
            dmask_ref[hh] = jnp.where(causal, jnp.exp(lg_ref[hh][:, :1] * dist), 0.0)

    idx = lax.broadcasted_iota(jnp.int32, (span, 1), 0).astype(F32)
    for hh in range(hg):
        lg = lg_ref[hh][:, :1]
        xi = jnp.exp(lg * (idx + 1.0))
        zeta = jnp.exp(lg * (span - 1.0 - idx))
        g_span = jnp.exp(lg * float(span))

        q = q_ref[:, hh * dk:(hh + 1) * dk]
        k = k_ref[:, hh * dk:(hh + 1) * dk]
        v = v_ref[:, hh * dv:(hh + 1) * dv]
        s = lax.dot_general(q, k, (((1,), (1,)), ((), ())), preferred_element_type=F32) * dmask_ref[hh]
        state = state_ref[hh]
        o = _dot(s.astype(BF16), v) + xi * _dot(q, state.astype(BF16))
        kz = (k.astype(F32) * zeta).astype(BF16)
        state_ref[hh] = state * g_span + lax.dot_general(kz, v, (((0,), (0,)), ((), ())),
                                                         preferred_element_type=F32)

        mu = jnp.mean(o, axis=-1, keepdims=True)
        oc = o - mu
        var = jnp.mean(oc * oc, axis=-1, keepdims=True)
        on = (oc * lax.rsqrt(var + EPS)) * gn_ref[:, hh * dv:(hh + 1) * dv]
        gate = _silu(g_ref[:, hh * dv:(hh + 1) * dv].astype(F32))
        o_ref[:, hh * dv:(hh + 1) * dv] = (gate * on).astype(BF16)


def _ret_core(qkvg, log_g, gn_g, nb, seq, heads, dk, dv, span=512, hg=4):
    t = qkvg.shape[0]
    ns = seq // span
    groups = heads // hg
    k_off = groups
    v_off = (2 * heads * dk) // (hg * dv)
    g_off = v_off + groups
    row = lambda b, h, i: b * ns + i
    return pl.pallas_call(
        functools.partial(_ret_core_kernel, span=span, hg=hg, dk=dk, dv=dv),
        grid=(nb, groups, ns),
        in_specs=[
            pl.BlockSpec((hg, 1, 128), lambda b, h, i: (h, 0, 0)),
            pl.BlockSpec((span, hg * dk), lambda b, h, i: (row(b, h, i), h)),
            pl.BlockSpec((span, hg * dk), lambda b, h, i: (row(b, h, i), k_off + h)),
            pl.BlockSpec((span, hg * dv), lambda b, h, i: (row(b, h, i), v_off + h)),
            pl.BlockSpec((span, hg * dv), lambda b, h, i: (row(b, h, i), g_off + h)),
            pl.BlockSpec((1, hg * dv), lambda b, h, i: (0, h)),
        ],
        out_specs=pl.BlockSpec((span, hg * dv), lambda b, h, i: (row(b, h, i), h)),
        out_shape=jax.ShapeDtypeStruct((t, heads * dv), BF16),
        scratch_shapes=[pltpu.VMEM((hg, dk, dv), F32), pltpu.VMEM((hg, span, span), F32)],
        compiler_params=_params("arbitrary", "arbitrary", "arbitrary"),
        name="ret_core",
    )(log_g, qkvg, qkvg, qkvg, qkvg, gn_g.reshape(1, heads * dv))


def _proj_res_kernel(y_ref, w_ref, x_ref, gt_ref, o_ref):
    o_ref[...] = x_ref[...] + gt_ref[...] * _dot(y_ref[...], w_ref[...])


def _proj_res(y, w, layer, x, mods, mod_idx, nb, tm=1024, tn=512):
    t, kdim = y.shape
    d = w.shape[2]
    bpb = (t // nb) // tm
    return pl.pallas_call(
        _proj_res_kernel,
        grid=(t // tm, d // tn),
        in_specs=[
            pl.BlockSpec((tm, kdim), lambda i, j: (i, 0)),
            pl.BlockSpec((None, kdim, tn), lambda i, j: (layer, 0, j)),
            pl.BlockSpec((tm, tn), lambda i, j: (i, j)),
            pl.BlockSpec((None, 1, tn), lambda i, j: (mod_idx * nb + i // bpb, 0, j)),
        ],
        out_specs=pl.BlockSpec((tm, tn), lambda i, j: (i, j)),
        out_shape=jax.ShapeDtypeStruct((t, d), F32),
        compiler_params=_params("arbitrary", "arbitrary"),
        name="proj_res",
    )(y, w, x, mods)


def _rope_table_kernel(pos_ref, inv_ref, cos_ref, sin_ref):
    cos, sin = _rope_tables(pos_ref, inv_ref)
    cos_ref[...] = cos
    sin_ref[...] = sin


def _mla_rope_tables(positions, half, tr=512):
    t = positions.size
    per_row = 128 // half
    rows = t // per_row
    pos_rep = jnp.repeat(positions.reshape(rows, per_row), half, axis=1)
    inv = ROPE_THETA ** (-jnp.arange(half, dtype=F32) / half)
    inv_rep = jnp.tile(inv, per_row).reshape(1, 128)
    cos, sin = pl.pallas_call(
        _rope_table_kernel,
        grid=(rows // tr,),
        in_specs=[pl.BlockSpec((tr, 128), lambda i: (i, 0)), pl.BlockSpec((1, 128), lambda i: (0, 0))],
        out_specs=[pl.BlockSpec((tr, 128), lambda i: (i, 0))] * 2,
        out_shape=[jax.ShapeDtypeStruct((rows, 128), F32)] * 2,
        compiler_params=_params("arbitrary"),
        name="mla_rope_tables",
    )(pos_rep, inv_rep)
    return cos.reshape(t, half), sin.reshape(t, half)


def _rope_tail(x, cos, sin):
    half = MLA_ROPE // 2
    x1 = x[:, :half]
    x2 = x[:, half:]
    return jnp.concatenate([x1 * cos - x2 * sin, x1 * sin + x2 * cos], axis=-1)


def _kv_kernel(x_ref, sh_ref, sc_ref, g_ref, cos_ref, sin_ref, wdc_ref, wdr_ref, lg_ref, wu_ref, k_ref, v_ref):
    h = _modulated_norm(x_ref[...], g_ref[...], sh_ref[...], sc_ref[...]).astype(BF16)
    ckv = _dot(h, wdc_ref[...])
    ms = jnp.mean(ckv * ckv, axis=-1, keepdims=True)
    ckv = ((ckv * lax.rsqrt(ms + EPS)) * lg_ref[...]).astype(BF16)
    kr = _rope_tail(_dot(h, wdr_ref[...]), cos_ref[...], sin_ref[...]).astype(BF16)
    width = MLA_NOPE + MLA_V
    for hd in range(MLA_HEADS):
        kv = _dot(ckv, wu_ref[:, hd * width:(hd + 1) * width])
        k_ref[hd, :, :MLA_NOPE] = kv[:, :MLA_NOPE].astype(BF16)
        k_ref[hd, :, MLA_NOPE:] = kr
        v_ref[hd] = kv[:, MLA_NOPE:].astype(BF16)


def _shared_kv(x, mods, nb, seq, norm_g, cos, sin, w_dc, w_dr, latent_g, w_ukv, tm=512):
    t, d = x.shape
    spb = seq // tm
    const = lambda i: (0, 0)
    return pl.pallas_call(
        _kv_kernel,
        grid=(t // tm,),
        in_specs=[
            pl.BlockSpec((tm, d), lambda i: (i, 0)),
            _mod_spec(0, nb, spb, d),
            _mod_spec(1, nb, spb, d),
            pl.BlockSpec((1, d), const),
            pl.BlockSpec((tm, MLA_ROPE // 2), lambda i: (i, 0)),
            pl.BlockSpec((tm, MLA_ROPE // 2), lambda i: (i, 0)),
            pl.BlockSpec(w_dc.shape, const),
            pl.BlockSpec(w_dr.shape, const),
            pl.BlockSpec((1, KV_RANK), const),
            pl.BlockSpec(w_ukv.shape, const),
        ],
        out_specs=[
            pl.BlockSpec((None, MLA_HEADS, tm, MLA_NOPE + MLA_ROPE), lambda i: (i // spb, 0, i % spb, 0)),
            pl.BlockSpec((None, MLA_HEADS, tm, MLA_V), lambda i: (i // spb, 0, i % spb, 0)),
        ],
        out_shape=[
            jax.ShapeDtypeStruct((nb, MLA_HEADS, seq, MLA_NOPE + MLA_ROPE), BF16),
            jax.ShapeDtypeStruct((nb, MLA_HEADS, seq, MLA_V), BF16),
        ],
        compiler_params=_params("arbitrary"),
        name="mla_kv",
    )(x, mods, mods, norm_g.reshape(1, d), cos, sin, w_dc, w_dr, latent_g.reshape(1, KV_RANK), w_ukv)


def _q_kernel(x_ref, sh_ref, sc_ref, g_ref, cos_ref, sin_ref, wd_ref, lg_ref, wu_ref, q_ref, *, q_scale):
    h = _modulated_norm(x_ref[...], g_ref[...], sh_ref[...], sc_ref[...]).astype(BF16)
    cq = _dot(h, wd_ref[...])
    ms = jnp.mean(cq * cq, axis=-1, keepdims=True)
    cq = ((cq * lax.rsqrt(ms + EPS)) * lg_ref[...]).astype(BF16)
    cos = cos_ref[...]
    sin = sin_ref[...]
    for hd in range(MLA_HEADS):
        qh = _dot(cq, wu_ref[hd]) * q_scale
        q_ref[hd, :, :MLA_NOPE] = qh[:, :MLA_NOPE].astype(BF16)
        q_ref[hd, :, MLA_NOPE:] = _rope_tail(qh[:, MLA_NOPE:], cos, sin).astype(BF16)


def _mla_q(x, mods, mod_base, nb, seq, norm_g, cos, sin, w_dq, latent_g, w_uq_heads, tm=512):
    t, d = x.shape
    spb = seq // tm
    const = lambda i: (0, 0)
    dq = MLA_NOPE + MLA_ROPE
    return pl.pallas_call(
        functools.partial(_q_kernel, q_scale=dq ** -0.5 * math.log2(math.e)),
        grid=(t // tm,),
        in_specs=[
            pl.BlockSpec((tm, d), lambda i: (i, 0)),
            _mod_spec(mod_base + 0, nb, spb, d),
            _mod_spec(mod_base + 1, nb, spb, d),
            pl.BlockSpec((1, d), const),
            pl.BlockSpec((tm, MLA_ROPE // 2), lambda i: (i, 0)),
            pl.BlockSpec((tm, MLA_ROPE // 2), lambda i: (i, 0)),
            pl.BlockSpec(w_dq.shape, const),
            pl.BlockSpec((1, Q_RANK), const),
            pl.BlockSpec(w_uq_heads.shape, lambda i: (0, 0, 0)),
        ],
        out_specs=pl.BlockSpec((None, MLA_HEADS, tm, dq), lambda i: (i // spb, 0, i % spb, 0)),
        out_shape=jax.ShapeDtypeStruct((nb, MLA_HEADS, seq, dq), BF16),
        compiler_params=_params("arbitrary"),
        name="mla_q",
    )(x, mods, mods, norm_g.reshape(1, d), cos, sin, w_dq, latent_g.reshape(1, Q_RANK), w_uq_heads)


def _attn_kernel(q_ref, k_ref, v_ref, mask_ref, o_ref, *, tq, nq, hg):
    qi = pl.program_id(2)
    nt = (((1,), (1,)), ((), ()))
    visible = mask_ref[...] != 0.0
    dv = v_ref.shape[-1]

    for i in range(nq):
        @pl.when(qi == i)
        def _(i=i):
            for hh in range(hg):
                q = q_ref[hh]
                m = l = acc = None
                for j in range(i, -1, -1):
                    lo = j * tq
                    s = lax.dot_general(q, k_ref[hh, lo:lo + tq, :], nt, preferred_element_type=F32)
                    if j == i:
                        s = jnp.where(visible, s, -1e30)
                    bm = jnp.max(s, axis=-1, keepdims=True)
                    m_new = bm if m is None else jnp.maximum(m, bm)
                    p = jnp.exp2(s - m_new)
                    ps = jnp.sum(p, axis=-1, keepdims=True)
                    pv = _dot(p.astype(BF16), v_ref[hh, lo:lo + tq, :])
                    if m is None:
                        l, acc = ps, pv
                    else:
                        alpha = jnp.exp2(m - m_new)
                        l = alpha * l + ps
                        acc = alpha * acc + pv
                    m = m_new
                o_ref[:, hh * dv:(hh + 1) * dv] = (acc / l).astype(BF16)


def _attention(q, k, v, tq=512, hg=2):
    nb, heads, seq, dq = q.shape
    dv = v.shape[-1]
    nq = seq // tq
    r = jnp.arange(tq, dtype=jnp.int32) // CHUNK
    mask = (r[None, :] <= r[:, None]).astype(F32)
    return pl.pallas_call(
        functools.partial(_attn_kernel, tq=tq, nq=nq, hg=hg),
        grid=(nb, heads // hg, nq),
        in_specs=[
            pl.BlockSpec((None, hg, tq, dq), lambda b, h, i: (b, h, i, 0)),
            pl.BlockSpec((None, hg, seq, dq), lambda b, h, i: (b, h, 0, 0)),
            pl.BlockSpec((None, hg, seq, dv), lambda b, h, i: (b, h, 0, 0)),
            pl.BlockSpec((tq, tq), lambda b, h, i: (0, 0)),
        ],
        out_specs=pl.BlockSpec((tq, hg * dv), lambda b, h, i: (b * nq + i, h)),
        out_shape=jax.ShapeDtypeStruct((nb * seq, heads * dv), BF16),
        compiler_params=_params("arbitrary", "arbitrary", "arbitrary"),
        name="mla_attn",
    )(q, k, v, mask)


def kernel(x, c, positions, ada_w, ada_b, norm_g, ffn_w_in, ffn_w_out, ret_w_in, ret_gn_g, ret_w_out, kv_ada_w, kv_ada_b, kv_norm_g, mla_w_dkv, kv_latent_g, mla_w_ukv, mla_w_dq, q_latent_g, mla_w_uq, mla_w_out, final_g):
    nb, seq, d = x.shape
    t = nb * seq
    depth = ada_w.shape[0]
    n_ret = ret_w_in.shape[0]
    ret_qk = d
    ret_dk = ret_qk // RET_HEADS
    ret_dv = ret_w_out.shape[1] // RET_HEADS

    xs = x.reshape(t, d)
    pos = positions.reshape(t, 1)
    c_pad = jnp.zeros((8, d), F32).at[:nb].set(c)

    half_ret = ret_dk // 2
    inv_ret = (ROPE_THETA ** (-jnp.arange(half_ret, dtype=F32) / half_ret)).reshape(1, half_ret)
    mla_cos, mla_sin = _mla_rope_tables(positions, MLA_ROPE // 2)
    log_g = jnp.log1p(-(2.0 ** (-5.0 - jnp.arange(RET_HEADS, dtype=F32))))
    log_g = jnp.broadcast_to(log_g[:, None, None], (RET_HEADS, 1, 128))

    ffn_order = [(l, k) for l in range(depth) for k in range(2)]
    ffn_w = (ffn_w_in[0, 0].astype(BF16), ffn_w_out[0, 0].astype(BF16))

    def ffn(xs, mods, mod_base, l, k, ffn_w):
        n = ffn_order.index((l, k))
        nxt = (ffn_w_in, ffn_w_out) + ffn_order[n + 1] if n + 1 < len(ffn_order) else None
        return _ffn(xs, mods, mod_base, nb, norm_g[l, 2 * k], ffn_w[0], ffn_w[1], final_g,
                    n + 1 == len(ffn_order), nxt)

    ret_in_w = ret_w_in.astype(BF16)
    ret_out_w = ret_w_out.astype(BF16)
    mla_out_w = mla_w_out.astype(BF16)

    kq = kk = vv = None
    for l in range(depth):
        mods = _mod_table(_mods(c_pad, ada_w, ada_b, l), nb, d)
        if l == n_ret:
            kv_mods = _mod_table(_mods(c_pad, kv_ada_w[None], kv_ada_b[None], 0), nb, d)
            w_dkv = mla_w_dkv.astype(BF16)
            kk, vv = _shared_kv(xs, kv_mods, nb, seq, kv_norm_g, mla_cos, mla_sin,
                                w_dkv[:, :KV_RANK], w_dkv[:, KV_RANK:], kv_latent_g, mla_w_ukv.astype(BF16))
        xs, ffn_w = ffn(xs, mods, 0, l, 0, ffn_w)
        if l < n_ret:
            qkvg = _ret_in(xs, mods, 3, nb, norm_g[l, 1], pos, inv_ret, ret_in_w, l, ret_qk, ret_dk)
            y = _ret_core(qkvg, log_g, ret_gn_g[l], nb, seq, RET_HEADS, ret_dk, ret_dv)
            xs = _proj_res(y, ret_out_w, l, xs, mods, 5, nb)
        else:
            jj = l - n_ret
            dq = MLA_NOPE + MLA_ROPE
            w_uq = mla_w_uq[jj].astype(BF16).reshape(Q_RANK, MLA_HEADS, dq).transpose(1, 0, 2)
            kq = _mla_q(xs, mods, 3, nb, seq, norm_g[l, 1], mla_cos, mla_sin, mla_w_dq[jj].astype(BF16),
                        q_latent_g[jj], w_uq)
            y = _attention(kq, kk, vv)
            xs = _proj_res(y, mla_out_w, jj, xs, mods, 5, nb, tn=1024)
        xs, ffn_w = ffn(xs, mods, 6, l, 1, ffn_w)
    return xs.reshape(nb, seq, d)
```

```python
import functools
import math

import jax
import jax.numpy as jnp
from jax import lax
from jax.experimental import pallas as pl
from jax.experimental.pallas import tpu as pltpu

CHUNK = 64
RET_HEADS = 8
MLA_HEADS = 16
MLA_NOPE = 128
MLA_ROPE = 64
MLA_V = 128
Q_RANK = 512
KV_RANK = 512
ROPE_THETA = 10000.0
EPS = 1e-6
N_MOD = 9

BF16 = jnp.bfloat16
F32 = jnp.float32

VMEM_LIMIT_BYTES = 56 * 1024 * 1024


def _params(*sem):
    return pltpu.CompilerParams(dimension_semantics=sem, vmem_limit_bytes=VMEM_LIMIT_BYTES)


def _silu(x):
    return x / (1.0 + jnp.exp(-x))


def _dot(a, b):
    return jnp.dot(a, b, preferred_element_type=F32)


def _modulated_norm(x, g, shift, scale):
    ms = jnp.mean(x * x, axis=-1, keepdims=True)
    return (x * lax.rsqrt(ms + EPS)) * g * (1.0 + scale) + shift


def _rope_tables(pos_ref, inv_ref):
    ang = pos_ref[...].astype(F32) * inv_ref[...]
    return jnp.cos(ang), jnp.sin(ang)


def _mod_spec(k, nb, blocks_per_batch, d):
    return pl.BlockSpec((None, 1, d), lambda i, *_: (k * nb + i // blocks_per_batch, 0, 0))


def _mods_kernel(c_ref, w_ref, b_ref, o_ref):
    ca = _silu(c_ref[...]).astype(BF16)
    o_ref[...] = _dot(ca, w_ref[...].astype(BF16)) + b_ref[...]


def _mods(c_pad, w, b, layer, tn=1024):
    rows, d = c_pad.shape
    nl, _, n = w.shape
    return pl.pallas_call(
        _mods_kernel,
        grid=(n // tn,),
        in_specs=[
            pl.BlockSpec((rows, d), lambda j: (0, 0)),
            pl.BlockSpec((None, d, tn), lambda j: (layer, 0, j)),
            pl.BlockSpec((None, 1, tn), lambda j: (layer, 0, j)),
        ],
        out_specs=pl.BlockSpec((rows, tn), lambda j: (0, j)),
        out_shape=jax.ShapeDtypeStruct((rows, n), F32),
        compiler_params=_params("arbitrary"),
        name="ada_mods",
    )(c_pad, w, b.reshape(nl, 1, n))


def _mod_table(m, nb, d):
    k = m.shape[1] // d
    return m[:nb].reshape(nb, k, d).transpose(1, 0, 2).reshape(k * nb, 1, d)


def _ffn_kernel(x_ref, sh_ref, sc_ref, gt_ref, g_ref, wg_ref, wu_ref, wo_ref, fg_ref, *rest,
                final_norm, row_splits, cast_next):
    if cast_next:
        nin_ref, nout_ref, o_ref, nin_bf_ref, nout_bf_ref, h_ref, inv_ref = rest
        nin_bf_ref[...] = nin_ref[...].astype(BF16)
        nout_bf_ref[...] = nout_ref[...].astype(BF16)
    else:
        o_ref, h_ref, inv_ref = rest
    j = pl.program_id(1)
    last = pl.num_programs(1) - 1
    rs = x_ref.shape[0] // row_splits
    slabs = [slice(r * rs, (r + 1) * rs) for r in range(row_splits)]

    def step(first):
        for rows in slabs:
            h = h_ref[rows, :]
            gate = _dot(h, wg_ref[...])
            up = _dot(h, wu_ref[...])
            a = (_silu(gate) * up).astype(BF16)
            y = _dot(a, wo_ref[...])
            if first:
                o_ref[rows, :] = y
            else:
                o_ref[rows, :] += y

    @pl.when(j == 0)
    def _():
        for rows in slabs:
            x = x_ref[rows, :]
            inv_ref[rows, :] = lax.rsqrt(jnp.mean(x * x, axis=-1, keepdims=True) + EPS)
            h = (x_ref[rows, :] * inv_ref[rows, :]) * g_ref[...] * (1.0 + sc_ref[...]) + sh_ref[...]
            h_ref[rows, :] = h.astype(BF16)
        step(True)

    @pl.when(jnp.logical_and(j > 0, j < last))
    def _():
        step(False)

    @pl.when(j == last)
    def _():
        step(False)
        for rows in slabs:
            y = x_ref[rows, :] + (0.5 * gt_ref[...]) * o_ref[rows, :]
            o_ref[rows, :] = y
            if final_norm:
                inv_ref[rows, :] = lax.rsqrt(jnp.mean(y * y, axis=-1, keepdims=True) + EPS)
                o_ref[rows, :] = (o_ref[rows, :] * inv_ref[rows, :]) * fg_ref[...]


def _ffn(x, mods, mod_base, nb, norm_g, w_in, w_out, final_g, final_norm, next_w=None, tm=1024, tf=512):
    t, d = x.shape
    dff = w_out.shape[0]
    nf = dff // tf
    assert nf >= 2, "the first and the last ff step are distinct branches"
    ni = t // tm
    bpb = (t // nb) // tm
    in_specs = [
        pl.BlockSpec((tm, d), lambda i, j: (i, 0)),
        _mod_spec(mod_base + 0, nb, bpb, d),
        _mod_spec(mod_base + 1, nb, bpb, d),
        _mod_spec(mod_base + 2, nb, bpb, d),
        pl.BlockSpec((1, d), lambda i, j: (0, 0)),
        pl.BlockSpec((d, tf), lambda i, j: (0, j)),
        pl.BlockSpec((d, tf), lambda i, j: (0, nf + j)),
        pl.BlockSpec((tf, d), lambda i, j: (j, 0)),
        pl.BlockSpec((1, d), lambda i, j: (0, 0)),
    ]
    args = [x, mods, mods, mods, norm_g.reshape(1, d), w_in, w_in, w_out, final_g.reshape(1, d)]
    out_specs = [pl.BlockSpec((tm, d), lambda i, j: (i, 0))]
    out_shape = [jax.ShapeDtypeStruct((t, d), F32)]
    if next_w is not None:
        nin, nout, nl, nk = next_w
        in_blk = (d // ni, 2 * dff // nf)
        out_blk = (dff // nf, d // ni)
        in_specs += [pl.BlockSpec((None, None) + in_blk, lambda i, j: (nl, nk, i, j)),
                     pl.BlockSpec((None, None) + out_blk, lambda i, j: (nl, nk, j, i))]
        args += [nin, nout]
        out_specs += [pl.BlockSpec(in_blk, lambda i, j: (i, j)), pl.BlockSpec(out_blk, lambda i, j: (j, i))]
        out_shape += [jax.ShapeDtypeStruct((d, 2 * dff), BF16), jax.ShapeDtypeStruct((dff, d), BF16)]
    res = pl.pallas_call(
        functools.partial(_ffn_kernel, final_norm=final_norm, row_splits=2, cast_next=next_w is not None),
        grid=(ni, nf),
        in_specs=in_specs,
        out_specs=out_specs,
        out_shape=out_shape,
        scratch_shapes=[pltpu.VMEM((tm, d), BF16), pltpu.VMEM((tm, 1), F32)],
        compiler_params=_params("arbitrary", "arbitrary"),
        name="ffn",
    )(*args)
    return res[0], (tuple(res[1:]) if next_w is not None else None)


def _ret_in_kernel(x_ref, sh_ref, sc_ref, g_ref, pos_ref, inv_ref, w_ref, o_ref, h_ref, cos_ref, sin_ref,
                   *, n_q_tiles, dk, q_scale, row_splits):
    j = pl.program_id(1)
    rs = x_ref.shape[0] // row_splits
    slabs = [slice(r * rs, (r + 1) * rs) for r in range(row_splits)]
    tn = o_ref.shape[1]
    half = dk // 2

    def project(rows, rope_scale):
        y = _dot(h_ref[rows, :], w_ref[...])
        if rope_scale is None:
            o_ref[rows, :] = y.astype(BF16)
            return
        cos = cos_ref[rows, :]
        sin = sin_ref[rows, :]
        for hd in range(tn // dk):
            x1 = y[:, hd * dk:hd * dk + half]
            x2 = y[:, hd * dk + half:(hd + 1) * dk]
            o_ref[rows, hd * dk:hd * dk + half] = ((x1 * cos - x2 * sin) * rope_scale).astype(BF16)
            o_ref[rows, hd * dk + half:(hd + 1) * dk] = ((x1 * sin + x2 * cos) * rope_scale).astype(BF16)

    @pl.when(j == 0)
    def _():
        for rows in slabs:
            h = _modulated_norm(x_ref[rows, :], g_ref[...], sh_ref[...], sc_ref[...])
            h_ref[rows, :] = h.astype(BF16)
            ang = pos_ref[rows, :].astype(F32) * inv_ref[...]
            cos_ref[rows, :] = jnp.cos(ang)
            sin_ref[rows, :] = jnp.sin(ang)
            project(rows, q_scale)

    @pl.when(jnp.logical_and(j > 0, j < n_q_tiles))
    def _():
        for rows in slabs:
            project(rows, q_scale)

    @pl.when(jnp.logical_and(j >= n_q_tiles, j < 2 * n_q_tiles))
    def _():
        for rows in slabs:
            project(rows, 1.0)

    @pl.when(j >= 2 * n_q_tiles)
    def _():
        for rows in slabs:
            project(rows, None)


def _ret_in(x, mods, mod_base, nb, norm_g, pos, inv, w, layer, qk_dim, dk, tm=1024, tn=1024):
    t, d = x.shape
    n = w.shape[2]
    bpb = (t // nb) // tm
    half = dk // 2
    return pl.pallas_call(
        functools.partial(_ret_in_kernel, n_q_tiles=qk_dim // tn, dk=dk, q_scale=dk ** -0.5, row_splits=4),
        grid=(t // tm, n // tn),
        in_specs=[
            pl.BlockSpec((tm, d), lambda i, j: (i, 0)),
            _mod_spec(mod_base + 0, nb, bpb, d),
            _mod_spec(mod_base + 1, nb, bpb, d),
            pl.BlockSpec((1, d), lambda i, j: (0, 0)),
            pl.BlockSpec((tm, 1), lambda i, j: (i, 0)),
            pl.BlockSpec((1, half), lambda i, j: (0, 0)),
            pl.BlockSpec((None, d, tn), lambda i, j: (layer, 0, j)),
        ],
        out_specs=pl.BlockSpec((tm, tn), lambda i, j: (i, j)),
        out_shape=jax.ShapeDtypeStruct((t, n), BF16),
        scratch_shapes=[pltpu.VMEM((tm, d), BF16), pltpu.VMEM((tm, half), F32), pltpu.VMEM((tm, half), F32)],
        compiler_params=_params("arbitrary", "arbitrary"),
        name="ret_in",
    )(x, mods, mods, norm_g.reshape(1, d), pos, inv, w)


def _ret_core_kernel(lg_ref, q_ref, k_ref, v_ref, g_ref, gn_ref, o_ref, state_ref, dmask_ref, *, span, hg, dk, dv):
    i = pl.program_id(2)

    @pl.when(i == 0)
    def _():
        state_ref[...] = jnp.zeros_like(state_ref)
        r = lax.broadcasted_iota(jnp.int32, (span, span), 0)
        c = lax.broadcasted_iota(jnp.int32, (span, span), 1)
        dist = jnp.abs(r - c).astype(F32)
        causal = (c // CHUNK) <= (r // CHUNK)
        for hh in range(hg):
            dmask_ref[hh] = jnp.where(causal, jnp.exp(lg_ref[hh][:, :1] * dist), 0.0)

    idx = lax.broadcasted_iota(jnp.int32, (span, 1), 0).astype(F32)
    for hh in range(hg):
        lg = lg_ref[hh][:, :1]
        xi = jnp.exp(lg * (idx + 1.0))
        zeta = jnp.exp(lg * (span - 1.0 - idx))
        g_span = jnp.exp(lg * float(span))

        q = q_ref[:, hh * dk:(hh + 1) * dk]
        k = k_ref[:, hh * dk:(hh + 1) * dk]
        v = v_ref[:, hh * dv:(hh + 1) * dv]
        s = lax.dot_general(q, k, (((1,), (1,)), ((), ())), preferred_element_type=F32) * dmask_ref[hh]
        state = state_ref[hh]
        o = _dot(s.astype(BF16), v) + xi * _dot(q, state.astype(BF16))
        kz = (k.astype(F32) * zeta).astype(BF16)
        state_ref[hh] = state * g_span + lax.dot_general(kz, v, (((0,), (0,)), ((), ())),
                                                         preferred_element_type=F32)

        mu = jnp.mean(o, axis=-1, keepdims=True)
        oc = o - mu
        var = jnp.mean(oc * oc, axis=-1, keepdims=True)
        on = (oc * lax.rsqrt(var + EPS)) * gn_ref[:, hh * dv:(hh + 1) * dv]
        gate = _silu(g_ref[:, hh * dv:(hh + 1) * dv].astype(F32))
        o_ref[:, hh * dv:(hh + 1) * dv] = (gate * on).astype(BF16)


def _ret_core(qkvg, log_g, gn_g, nb, seq, heads, dk, dv, span=512, hg=4):
    t = qkvg.shape[0]
    ns = seq // span
    groups = heads // hg
    k_off = groups
    v_off = (2 * heads * dk) // (hg * dv)
    g_off = v_off + groups
    row = lambda b, h, i: b * ns + i
    return pl.pallas_call(
        functools.partial(_ret_core_kernel, span=span, hg=hg, dk=dk, dv=dv),
        grid=(nb, groups, ns),
        in_specs=[
            pl.BlockSpec((hg, 1, 128), lambda b, h, i: (h, 0, 0)),
            pl.BlockSpec((span, hg * dk), lambda b, h, i: (row(b, h, i), h)),
            pl.BlockSpec((span, hg * dk), lambda b, h, i: (row(b, h, i), k_off + h)),
            pl.BlockSpec((span, hg * dv), lambda b, h, i: (row(b, h, i), v_off + h)),
            pl.BlockSpec((span, hg * dv), lambda b, h, i: (row(b, h, i), g_off + h)),
            pl.BlockSpec((1, hg * dv), lambda b, h, i: (0, h)),
        ],
        out_specs=pl.BlockSpec((span, hg * dv), lambda b, h, i: (row(b, h, i), h)),
        out_shape=jax.ShapeDtypeStruct((t, heads * dv), BF16),
        scratch_shapes=[pltpu.VMEM((hg, dk, dv), F32), pltpu.VMEM((hg, span, span), F32)],
        compiler_params=_params("arbitrary", "arbitrary", "arbitrary"),
        name="ret_core",
    )(log_g, qkvg, qkvg, qkvg, qkvg, gn_g.reshape(1, heads * dv))


def _proj_res_kernel(y_ref, w_ref, x_ref, gt_ref, o_ref):
    o_ref[...] = x_ref[...] + gt_ref[...] * _dot(y_ref[...], w_ref[...])


def _proj_res(y, w, layer, x, mods, mod_idx, nb, tm=1024, tn=512):
    t, kdim = y.shape
    d = w.shape[2]
    bpb = (t // nb) // tm
    return pl.pallas_call(
        _proj_res_kernel,
        grid=(t // tm, d // tn),
        in_specs=[
            pl.BlockSpec((tm, kdim), lambda i, j: (i, 0)),
            pl.BlockSpec((None, kdim, tn), lambda i, j: (layer, 0, j)),
            pl.BlockSpec((tm, tn), lambda i, j: (i, j)),
            pl.BlockSpec((None, 1, tn), lambda i, j: (mod_idx * nb + i // bpb, 0, j)),
        ],
        out_specs=pl.BlockSpec((tm, tn), lambda i, j: (i, j)),
        out_shape=jax.ShapeDtypeStruct((t, d), F32),
        compiler_params=_params("arbitrary", "arbitrary"),
        name="proj_res",
    )(y, w, x, mods)


def _rope_table_kernel(pos_ref, inv_ref, cos_ref, sin_ref):
    cos, sin = _rope_tables(pos_ref, inv_ref)
    cos_ref[...] = cos
    sin_ref[...] = sin


def _mla_rope_tables(positions, half, tr=512):
    t = positions.size
    per_row = 128 // half
    rows = t // per_row
    pos_rep = jnp.repeat(positions.reshape(rows, per_row), half, axis=1)
    inv = ROPE_THETA ** (-jnp.arange(half, dtype=F32) / half)
    inv_rep = jnp.tile(inv, per_row).reshape(1, 128)
    cos, sin = pl.pallas_call(
        _rope_table_kernel,
        grid=(rows // tr,),
        in_specs=[pl.BlockSpec((tr, 128), lambda i: (i, 0)), pl.BlockSpec((1, 128), lambda i: (0, 0))],
        out_specs=[pl.BlockSpec((tr, 128), lambda i: (i, 0))] * 2,
        out_shape=[jax.ShapeDtypeStruct((rows, 128), F32)] * 2,
        compiler_params=_params("arbitrary"),
        name="mla_rope_tables",
    )(pos_rep, inv_rep)
    return cos.reshape(t, half), sin.reshape(t, half)


def _rope_tail(x, cos, sin):
    half = MLA_ROPE // 2
    x1 = x[:, :half]
    x2 = x[:, half:]
    return jnp.concatenate([x1 * cos - x2 * sin, x1 * sin + x2 * cos], axis=-1)


def _kv_kernel(x_ref, sh_ref, sc_ref, g_ref, cos_ref, sin_ref, wdc_ref, wdr_ref, lg_ref, wu_ref, k_ref, v_ref):
    h = _modulated_norm(x_ref[...], g_ref[...], sh_ref[...], sc_ref[...]).astype(BF16)
    ckv = _dot(h, wdc_ref[...])
    ms = jnp.mean(ckv * ckv, axis=-1, keepdims=True)
    ckv = ((ckv * lax.rsqrt(ms + EPS)) * lg_ref[...]).astype(BF16)
    kr = _rope_tail(_dot(h, wdr_ref[...]), cos_ref[...], sin_ref[...]).astype(BF16)
    width = MLA_NOPE + MLA_V
    for hd in range(MLA_HEADS):
        kv = _dot(ckv, wu_ref[:, hd * width:(hd + 1) * width])
        k_ref[hd, :, :MLA_NOPE] = kv[:, :MLA_NOPE].astype(BF16)
        k_ref[hd, :, MLA_NOPE:] = kr
        v_ref[hd] = kv[:, MLA_NOPE:].astype(BF16)


def _shared_kv(x, mods, nb, seq, norm_g, cos, sin, w_dc, w_dr, latent_g, w_ukv, tm=512):
    t, d = x.shape
    spb = seq // tm
    const = lambda i: (0, 0)
    return pl.pallas_call(
        _kv_kernel,
        grid=(t // tm,),
        in_specs=[
            pl.BlockSpec((tm, d), lambda i: (i, 0)),
            _mod_spec(0, nb, spb, d),
            _mod_spec(1, nb, spb, d),
            pl.BlockSpec((1, d), const),
            pl.BlockSpec((tm, MLA_ROPE // 2), lambda i: (i, 0)),
            pl.BlockSpec((tm, MLA_ROPE // 2), lambda i: (i, 0)),
            pl.BlockSpec(w_dc.shape, const),
            pl.BlockSpec(w_dr.shape, const),
            pl.BlockSpec((1, KV_RANK), const),
            pl.BlockSpec(w_ukv.shape, const),
        ],
        out_specs=[
            pl.BlockSpec((None, MLA_HEADS, tm, MLA_NOPE + MLA_ROPE), lambda i: (i // spb, 0, i % spb, 0)),
            pl.BlockSpec((None, MLA_HEADS, tm, MLA_V), lambda i: (i // spb, 0, i % spb, 0)),
        ],
        out_shape=[
            jax.ShapeDtypeStruct((nb, MLA_HEADS, seq, MLA_NOPE + MLA_ROPE), BF16),
            jax.ShapeDtypeStruct((nb, MLA_HEADS, seq, MLA_V), BF16),
        ],
        compiler_params=_params("arbitrary"),
        name="mla_kv",
    )(x, mods, mods, norm_g.reshape(1, d), cos, sin, w_dc, w_dr, latent_g.reshape(1, KV_RANK), w_ukv)


def _q_kernel(x_ref, sh_ref, sc_ref, g_ref, cos_ref, sin_ref, wd_ref, lg_ref, wu_ref, q_ref, *, q_scale):
    h = _modulated_norm(x_ref[...], g_ref[...], sh_ref[...], sc_ref[...]).astype(BF16)
    cq = _dot(h, wd_ref[...])
    ms = jnp.mean(cq * cq, axis=-1, keepdims=True)
    cq = ((cq * lax.rsqrt(ms + EPS)) * lg_ref[...]).astype(BF16)
    cos = cos_ref[...]
    sin = sin_ref[...]
    for hd in range(MLA_HEADS):
        qh = _dot(cq, wu_ref[hd]) * q_scale
        q_ref[hd, :, :MLA_NOPE] = qh[:, :MLA_NOPE].astype(BF16)
        q_ref[hd, :, MLA_NOPE:] = _rope_tail(qh[:, MLA_NOPE:], cos, sin).astype(BF16)


def _mla_q(x, mods, mod_base, nb, seq, norm_g, cos, sin, w_dq, latent_g, w_uq_heads, tm=512):
    t, d = x.shape
    spb = seq // tm
    const = lambda i: (0, 0)
    dq = MLA_NOPE + MLA_ROPE
    return pl.pallas_call(
        functools.partial(_q_kernel, q_scale=dq ** -0.5 * math.log2(math.e)),
        grid=(t // tm,),
        in_specs=[
            pl.BlockSpec((tm, d), lambda i: (i, 0)),
            _mod_spec(mod_base + 0, nb, spb, d),
            _mod_spec(mod_base + 1, nb, spb, d),
            pl.BlockSpec((1, d), const),
            pl.BlockSpec((tm, MLA_ROPE // 2), lambda i: (i, 0)),
            pl.BlockSpec((tm, MLA_ROPE // 2), lambda i: (i, 0)),
            pl.BlockSpec(w_dq.shape, const),
            pl.BlockSpec((1, Q_RANK), const),
            pl.BlockSpec(w_uq_heads.shape, lambda i: (0, 0, 0)),
        ],
        out_specs=pl.BlockSpec((None, MLA_HEADS, tm, dq), lambda i: (i // spb, 0, i % spb, 0)),
        out_shape=jax.ShapeDtypeStruct((nb, MLA_HEADS, seq, dq), BF16),
        compiler_params=_params("arbitrary"),
        name="mla_q",
    )(x, mods, mods, norm_g.reshape(1, d), cos, sin, w_dq, latent_g.reshape(1, Q_RANK), w_uq_heads)


def _attn_kernel(q_ref, k_ref, v_ref, mask_ref, o_ref, *, tq, nq, hg):
    qi = pl.program_id(2)
    nt = (((1,), (1,)), ((), ()))
    visible = mask_ref[...] != 0.0
    dv = v_ref.shape[-1]

    for i in range(nq):
        @pl.when(qi == i)
        def _(i=i):
            for hh in range(hg):
                q = q_ref[hh]
                m = l = acc = None
                for j in range(i, -1, -1):
                    lo = j * tq
                    s = lax.dot_general(q, k_ref[hh, lo:lo + tq, :], nt, preferred_element_type=F32)
                    if j == i:
                        s = jnp.where(visible, s, -1e30)
                    bm = jnp.max(s, axis=-1, keepdims=True)
                    m_new = bm if m is None else jnp.maximum(m, bm)
                    p = jnp.exp2(s - m_new)
                    ps = jnp.sum(p, axis=-1, keepdims=True)
                    pv = _dot(p.astype(BF16), v_ref[hh, lo:lo + tq, :])
                    if m is None:
                        l, acc = ps, pv
                    else:
                        alpha = jnp.exp2(m - m_new)
                        l = alpha * l + ps
                        acc = alpha * acc + pv
                    m = m_new
                o_ref[:, hh * dv:(hh + 1) * dv] = (acc / l).astype(BF16)


def _attention(q, k, v, tq=512, hg=2):
    nb, heads, seq, dq = q.shape
    dv = v.shape[-1]
    nq = seq // tq
    r = jnp.arange(tq, dtype=jnp.int32) // CHUNK
    mask = (r[None, :] <= r[:, None]).astype(F32)
    return pl.pallas_call(
        functools.partial(_attn_kernel, tq=tq, nq=nq, hg=hg),
        grid=(nb, heads // hg, nq),
        in_specs=[
            pl.BlockSpec((None, hg, tq, dq), lambda b, h, i: (b, h, i, 0)),
            pl.BlockSpec((None, hg, seq, dq), lambda b, h, i: (b, h, 0, 0)),
            pl.BlockSpec((None, hg, seq, dv), lambda b, h, i: (b, h, 0, 0)),
            pl.BlockSpec((tq, tq), lambda b, h, i: (0, 0)),
        ],
        out_specs=pl.BlockSpec((tq, hg * dv), lambda b, h, i: (b * nq + i, h)),
        out_shape=jax.ShapeDtypeStruct((nb * seq, heads * dv), BF16),
        compiler_params=_params("arbitrary", "arbitrary", "arbitrary"),
        name="mla_attn",
    )(q, k, v, mask)


def kernel(x, c, positions, ada_w, ada_b, norm_g, ffn_w_in, ffn_w_out, ret_w_in, ret_gn_g, ret_w_out, kv_ada_w, kv_ada_b, kv_norm_g, mla_w_dkv, kv_latent_g, mla_w_ukv, mla_w_dq, q_latent_g, mla_w_uq, mla_w_out, final_g):
    nb, seq, d = x.shape
    t = nb * seq
    depth = ada_w.shape[0]
    n_ret = ret_w_in.shape[0]
    ret_qk = d
    ret_dk = ret_qk // RET_HEADS
    ret_dv = ret_w_out.shape[1] // RET_HEADS

    xs = x.reshape(t, d)
    pos = positions.reshape(t, 1)
    c_pad = jnp.zeros((8, d), F32).at[:nb].set(c)

    half_ret = ret_dk // 2
    inv_ret = (ROPE_THETA ** (-jnp.arange(half_ret, dtype=F32) / half_ret)).reshape(1, half_ret)
    mla_cos, mla_sin = _mla_rope_tables(positions, MLA_ROPE // 2)
    log_g = jnp.log1p(-(2.0 ** (-5.0 - jnp.arange(RET_HEADS, dtype=F32))))
    log_g = jnp.broadcast_to(log_g[:, None, None], (RET_HEADS, 1, 128))

    ffn_order = [(l, k) for l in range(depth) for k in range(2)]
    ffn_w = (ffn_w_in[0, 0].astype(BF16), ffn_w_out[0, 0].astype(BF16))

    def ffn(xs, mods, mod_base, l, k, ffn_w):
        n = ffn_order.index((l, k))
        nxt = (ffn_w_in, ffn_w_out) + ffn_order[n + 1] if n + 1 < len(ffn_order) else None
        return _ffn(xs, mods, mod_base, nb, norm_g[l, 2 * k], ffn_w[0], ffn_w[1], final_g,
                    n + 1 == len(ffn_order), nxt)

    ret_in_w = ret_w_in.astype(BF16)
    ret_out_w = ret_w_out.astype(BF16)
    mla_out_w = mla_w_out.astype(BF16)

    kq = kk = vv = None
    for l in range(depth):
        mods = _mod_table(_mods(c_pad, ada_w, ada_b, l), nb, d)
        if l == n_ret:
            kv_mods = _mod_table(_mods(c_pad, kv_ada_w[None], kv_ada_b[None], 0), nb, d)
            w_dkv = mla_w_dkv.astype(BF16)
            kk, vv = _shared_kv(xs, kv_mods, nb, seq, kv_norm_g, mla_cos, mla_sin,
                                w_dkv[:, :KV_RANK], w_dkv[:, KV_RANK:], kv_latent_g, mla_w_ukv.astype(BF16))
        xs, ffn_w = ffn(xs, mods, 0, l, 0, ffn_w)
        if l < n_ret:
            qkvg = _ret_in(xs, mods, 3, nb, norm_g[l, 1], pos, inv_ret, ret_in_w, l, ret_qk, ret_dk)
            y = _ret_core(qkvg, log_g, ret_gn_g[l], nb, seq, RET_HEADS, ret_dk, ret_dv)
            xs = _proj_res(y, ret_out_w, l, xs, mods, 5, nb)
        else:
            jj = l - n_ret
            dq = MLA_NOPE + MLA_ROPE
            w_uq = mla_w_uq[jj].astype(BF16).reshape(Q_RANK, MLA_HEADS, dq).transpose(1, 0, 2)
            kq = _mla_q(xs, mods, 3, nb, seq, norm_g[l, 1], mla_cos, mla_sin, mla_w_dq[jj].astype(BF16),
                        q_latent_g[jj], w_uq)
            y = _attention(kq, kk, vv)
            xs = _proj_res(y, mla_out_w, jj, xs, mods, 5, nb, tn=1024)
        xs, ffn_w = ffn(xs, mods, 6, l, 1, ffn_w)
    return xs.reshape(nb, seq, d)
```
